```python
import math
import jax, jax.numpy as jnp
from jax import lax
import numpy as np

D_MODEL = 2048
BATCH = 4
SEQ = 8192
DEPTH = 2
DEC_BATCH = 8
DEC_SEQ = 2048
PAST_LEN = 128

N_MIXERS = 2
N_S5_LAYERS = (DEPTH + N_MIXERS - 1) // N_MIXERS
N_ML_LAYERS = DEPTH // N_MIXERS
N_DIR = 2
D_FF = 5632
CHUNK = 128
EPS = 1e-6
S5_WIDTH = D_MODEL
S5_GROUP = 16
S5_GROUPS = S5_WIDTH // S5_GROUP
S5_STATE = 64
ML_INNER = 2 * D_MODEL
ML_HEADS = 16
ML_HEAD_DIM = ML_INNER // ML_HEADS
ML_QKV_BLOCK = 4
ML_CONV = 5

kernel_name = "bidir_s5_mlstm_macaron_encoder"


def _rmsnorm(x, g):
    xf = x.astype(jnp.float32)
    y = xf * lax.rsqrt(jnp.mean(xf * xf, axis=-1, keepdims=True) + EPS)
    return (y * g.astype(jnp.float32)).astype(x.dtype)


def _swiglu(x, w_in, w_out):
    gate, up = jnp.split(x @ w_in, 2, axis=-1)
    return (jax.nn.silu(gate) * up) @ w_out


def _s5_combine(left, right):
    a_l, b_l = left
    a_r, b_r = right
    return a_l * a_r, a_r * b_l + b_r


def _s5_direction(u, lam_bar, b_bar, c):
    bsz, length = u.shape[0], u.shape[1]
    uc = jnp.moveaxis(u.reshape(bsz, length // CHUNK, CHUNK, S5_GROUPS, S5_GROUP), 1, 0)
    a = jnp.broadcast_to(lam_bar, (bsz, CHUNK, S5_GROUPS, S5_STATE))

    def step(h, u_blk):
        bu = jnp.einsum('btgc,gpc->btgp', u_blk.astype(jnp.complex64), b_bar)
        a_cum, hs = lax.associative_scan(_s5_combine, (a, bu), axis=1)
        hs = hs + a_cum * h[:, None]
        y = jnp.einsum('btgp,gcp->btgc', hs, c).real
        return hs[:, -1], y

    h0 = jnp.zeros((bsz, S5_GROUPS, S5_STATE), jnp.complex64)
    _, ys = lax.scan(step, h0, uc)
    return jnp.moveaxis(ys, 0, 1).reshape(bsz, length, S5_GROUPS, S5_GROUP)


def _s5_mixer(x, w_in, lam_re, lam_im, log_step, b_re, b_im, c_re, c_im, d, w_glu):
    f32 = jnp.float32
    bsz, length, _ = x.shape
    u = (x @ w_in).astype(f32)
    ug = u.reshape(bsz, length, S5_GROUPS, S5_GROUP)
    y = d.astype(f32) * u
    for direction in range(N_DIR):
        lam = lax.complex(jnp.minimum(lam_re[direction].astype(f32), -1e-4), lam_im[direction].astype(f32))
        delta = jnp.exp(log_step[direction].astype(f32))[:, None]
        lam_bar = jnp.exp(lam * delta)
        b = lax.complex(b_re[direction].astype(f32), b_im[direction].astype(f32))
        b_bar = ((lam_bar - 1.0) / lam)[..., None] * b
        c = lax.complex(c_re[direction].astype(f32), c_im[direction].astype(f32))
        if direction == 0:
            yd = _s5_direction(ug, lam_bar, b_bar, c)
        else:
            yd = _s5_direction(ug[:, ::-1], lam_bar, b_bar, c)[:, ::-1]
        y = y + yd.reshape(bsz, length, S5_WIDTH)
    y = jax.nn.gelu(y).astype(x.dtype)
    val, gate = jnp.split(y @ w_glu, 2, axis=-1)
    return val * jax.nn.sigmoid(gate)


def _mlstm_direction(q, k, v, log_i, log_f):
    bsz, nh, length, dh = q.shape
    nc = length // CHUNK

    def chunks(t):
        return jnp.moveaxis(t.reshape((bsz, nh, nc, CHUNK) + t.shape[3:]), 2, 0)

    mask = jnp.tril(jnp.ones((CHUNK, CHUNK), dtype=bool))

    def step(carry, blk):
        c_mat, n_vec, m = carry
        qb, kb, vb, li, lf = blk
        bcum = jnp.cumsum(lf, axis=-1)
        g = bcum[..., -1]
        a = bcum + m[..., None]
        dmat = jnp.where(mask, bcum[..., :, None] - bcum[..., None, :] + li[..., None, :], -jnp.inf)
        m_t = jnp.maximum(a, jnp.max(dmat, axis=-1))
        s = jnp.einsum('bhtd,bhsd->bhts', qb, kb) * jnp.exp(dmat - m_t[..., None])
        e = jnp.exp(a - m_t)
        num = e[..., None] * jnp.einsum('bhvk,bhtk->bhtv', c_mat, qb) + jnp.einsum('bhts,bhsv->bhtv', s, vb)
        den = e * jnp.einsum('bhk,bhtk->bht', n_vec, qb) + jnp.sum(s, axis=-1)
        h = num / jnp.maximum(jnp.abs(den), jnp.exp(-m_t))[..., None]
        r = g[..., None] - bcum + li
        m_new = jnp.maximum(g + m, jnp.max(r, axis=-1))
        decay = jnp.exp(g + m - m_new)
        wr = jnp.exp(r - m_new[..., None])
        c_new = decay[..., None, None] * c_mat + jnp.einsum('bhsv,bhsk->bhvk', vb * wr[..., None], kb)
        n_new = decay[..., None] * n_vec + jnp.einsum('bhs,bhsk->bhk', wr, kb)
        return (c_new, n_new, m_new), h

    init = (jnp.zeros((bsz, nh, dh, dh), jnp.float32), jnp.zeros((bsz, nh, dh), jnp.float32), jnp.zeros((bsz, nh), jnp.float32))
    _, hs = lax.scan(step, init, (chunks(q), chunks(k), chunks(v), chunks(log_i), chunks(log_f)))
    return jnp.moveaxis(hs, 0, 2).reshape(bsz, nh, length, dh)


def _mlstm_mixer(x, w_in, conv_w, conv_b, wq, wk, wv, w_gates, b_gates, norm_g, skip, w_out):
    f32 = jnp.float32
    bsz, length, _ = x.shape
    xm, z = jnp.split(x @ w_in, 2, axis=-1)
    xc = lax.conv_general_dilated(xm, conv_w[:, None, :], window_strides=(1,), padding=[(ML_CONV // 2, ML_CONV // 2)], dimension_numbers=('NWC', 'WIO', 'NWC'), feature_group_count=ML_INNER)
    xc = jax.nn.silu(xc + conv_b)

    def blockdiag(t, w):
        return jnp.einsum('blnc,ncd->blnd', t.reshape(bsz, length, ML_INNER // ML_QKV_BLOCK, ML_QKV_BLOCK), w).reshape(bsz, length, ML_INNER)

    q = blockdiag(xc, wq)
    k = blockdiag(xc, wk)
    v = blockdiag(xm, wv)
    gates = (q @ w_gates[0] + k @ w_gates[1] + v @ w_gates[2] + b_gates).astype(f32)
    gates = jnp.moveaxis(gates.reshape(bsz, length, N_DIR, 2, ML_HEADS), 1, -1)

    def heads(t):
        return t.reshape(bsz, length, ML_HEADS, ML_HEAD_DIM).transpose(0, 2, 1, 3).astype(f32)

    qh = heads(q)
    kh = heads(k) * (ML_HEAD_DIM ** -0.5)
    vh = heads(v)
    h_fw = _mlstm_direction(qh, kh, vh, gates[:, 0, 0], jax.nn.log_sigmoid(gates[:, 0, 1]))
    h_bw = _mlstm_direction(qh[:, :, ::-1], kh[:, :, ::-1], vh[:, :, ::-1], gates[:, 1, 0, :, ::-1], jax.nn.log_sigmoid(gates[:, 1, 1, :, ::-1]))[:, :, ::-1]
    h = h_fw + h_bw
    mu = jnp.mean(h, axis=-1, keepdims=True)
    var = jnp.mean(jnp.square(h - mu), axis=-1, keepdims=True)
    hn = ((h - mu) * lax.rsqrt(var + EPS)).transpose(0, 2, 1, 3).reshape(bsz, length, ML_INNER) * norm_g.astype(f32)
    out = jax.nn.sigmoid(z.astype(f32)) * (hn + skip.astype(f32) * xc.astype(f32))
    return out.astype(x.dtype) @ w_out


def _trunk(x, norm_g, final_g, ffn_w_in, ffn_w_out, s5_w_in, s5_lambda_re, s5_lambda_im, s5_log_step, s5_b_re, s5_b_im, s5_c_re, s5_c_im, s5_d, s5_w_glu, ml_w_in, ml_conv_w, ml_conv_b, ml_wq, ml_wk, ml_wv, ml_w_gates, ml_b_gates, ml_norm_g, ml_skip, ml_w_out):
    for layer in range(DEPTH):
        j = layer // N_MIXERS
        x = x + 0.5 * _swiglu(_rmsnorm(x, norm_g[layer, 0]), ffn_w_in[layer, 0], ffn_w_out[layer, 0])
        hn = _rmsnorm(x, norm_g[layer, 1])
        if layer % N_MIXERS == 0:
            x = x + _s5_mixer(hn, s5_w_in[j], s5_lambda_re[j], s5_lambda_im[j], s5_log_step[j], s5_b_re[j], s5_b_im[j], s5_c_re[j], s5_c_im[j], s5_d[j], s5_w_glu[j])
        else:
            x = x + _mlstm_mixer(hn, ml_w_in[j], ml_conv_w[j], ml_conv_b[j], ml_wq[j], ml_wk[j], ml_wv[j], ml_w_gates[j], ml_b_gates[j], ml_norm_g[j], ml_skip[j], ml_w_out[j])
        x = x + 0.5 * _swiglu(_rmsnorm(x, norm_g[layer, 2]), ffn_w_in[layer, 1], ffn_w_out[layer, 1])
    return _rmsnorm(x, final_g)


def setup_inputs(seed: int = 0) -> dict:
    key = jax.random.key(seed)
    ks = iter(jax.random.split(key, 40))
    f32 = jnp.float32

    def nrm(shape, scale):
        return scale * jax.random.normal(next(ks), shape, f32)

    x_prompt = nrm((BATCH, SEQ, D_MODEL), 1.0)
    x_sample = nrm((DEC_BATCH, DEC_SEQ, D_MODEL), 1.0)
    norm_g = 1.0 + nrm((DEPTH, 3, D_MODEL), 0.01)
    final_g = 1.0 + nrm((D_MODEL,), 0.01)
    ffn_w_in = nrm((DEPTH, 2, D_MODEL, 2 * D_FF), D_MODEL ** -0.5)
    ffn_w_out = nrm((DEPTH, 2, D_FF, D_MODEL), D_FF ** -0.5)
    s5_w_in = nrm((N_S5_LAYERS, D_MODEL, S5_WIDTH), D_MODEL ** -0.5)
    s5_lambda_re = -0.5 + nrm((N_S5_LAYERS, N_DIR, S5_GROUPS, S5_STATE), 0.01)
    s5_lambda_im = jnp.pi * jnp.arange(S5_STATE, dtype=f32) + nrm((N_S5_LAYERS, N_DIR, S5_GROUPS, S5_STATE), 0.01)
    s5_log_step = jax.random.uniform(next(ks), (N_S5_LAYERS, N_DIR, S5_GROUPS), f32, math.log(1e-3), math.log(1e-1))
    s5_b_re = nrm((N_S5_LAYERS, N_DIR, S5_GROUPS, S5_STATE, S5_GROUP), (2 * S5_GROUP) ** -0.5)
    s5_b_im = nrm((N_S5_LAYERS, N_DIR, S5_GROUPS, S5_STATE, S5_GROUP), (2 * S5_GROUP) ** -0.5)
    s5_c_re = nrm((N_S5_LAYERS, N_DIR, S5_GROUPS, S5_GROUP, S5_STATE), S5_STATE ** -0.5)
    s5_c_im = nrm((N_S5_LAYERS, N_DIR, S5_GROUPS, S5_GROUP, S5_STATE), S5_STATE ** -0.5)
    s5_d = nrm((N_S5_LAYERS, S5_WIDTH), 1.0)
    s5_w_glu = nrm((N_S5_LAYERS, S5_WIDTH, 2 * D_MODEL), S5_WIDTH ** -0.5)
    ml_w_in = nrm((N_ML_LAYERS, D_MODEL, 2 * ML_INNER), D_MODEL ** -0.5)
    ml_conv_w = nrm((N_ML_LAYERS, ML_CONV, ML_INNER), ML_CONV ** -0.5)
    ml_conv_b = nrm((N_ML_LAYERS, ML_INNER), 0.01)
    ml_wq = nrm((N_ML_LAYERS, ML_INNER // ML_QKV_BLOCK, ML_QKV_BLOCK, ML_QKV_BLOCK), ML_QKV_BLOCK ** -0.5)
    ml_wk = nrm((N_ML_LAYERS, ML_INNER // ML_QKV_BLOCK, ML_QKV_BLOCK, ML_QKV_BLOCK), ML_QKV_BLOCK ** -0.5)
    ml_wv = nrm((N_ML_LAYERS, ML_INNER // ML_QKV_BLOCK, ML_QKV_BLOCK, ML_QKV_BLOCK), ML_QKV_BLOCK ** -0.5)
    ml_w_gates = nrm((N_ML_LAYERS, 3, ML_INNER, N_DIR * 2 * ML_HEADS), (3 * ML_INNER) ** -0.5)
    i_bias = nrm((N_ML_LAYERS, N_DIR, 1, ML_HEADS), 0.1)
    f_bias = jnp.linspace(3.0, 6.0, ML_HEADS, dtype=f32) + nrm((N_ML_LAYERS, N_DIR, 1, ML_HEADS), 0.01)
    ml_b_gates = jnp.concatenate([i_bias, f_bias], axis=2).reshape(N_ML_LAYERS, N_DIR * 2 * ML_HEADS)
    ml_norm_g = 1.0 + nrm((N_ML_LAYERS, ML_INNER), 0.01)
    ml_skip = 1.0 + nrm((N_ML_LAYERS, ML_INNER), 0.01)
    ml_w_out = nrm((N_ML_LAYERS, ML_INNER, D_MODEL), ML_INNER ** -0.5)
    return {"x_prompt": x_prompt, "x_sample": x_sample, "norm_g": norm_g, "final_g": final_g, "ffn_w_in": ffn_w_in, "ffn_w_out": ffn_w_out, "s5_w_in": s5_w_in, "s5_lambda_re": s5_lambda_re, "s5_lambda_im": s5_lambda_im, "s5_log_step": s5_log_step, "s5_b_re": s5_b_re, "s5_b_im": s5_b_im, "s5_c_re": s5_c_re, "s5_c_im": s5_c_im, "s5_d": s5_d, "s5_w_glu": s5_w_glu, "ml_w_in": ml_w_in, "ml_conv_w": ml_conv_w, "ml_conv_b": ml_conv_b, "ml_wq": ml_wq, "ml_wk": ml_wk, "ml_wv": ml_wv, "ml_w_gates": ml_w_gates, "ml_b_gates": ml_b_gates, "ml_norm_g": ml_norm_g, "ml_skip": ml_skip, "ml_w_out": ml_w_out}


def reference(x_prompt, x_sample, norm_g, final_g, ffn_w_in, ffn_w_out, s5_w_in, s5_lambda_re, s5_lambda_im, s5_log_step, s5_b_re, s5_b_im, s5_c_re, s5_c_im, s5_d, s5_w_glu, ml_w_in, ml_conv_w, ml_conv_b, ml_wq, ml_wk, ml_wv, ml_w_gates, ml_b_gates, ml_norm_g, ml_skip, ml_w_out):
    y_prompt = _trunk(x_prompt, norm_g, final_g, ffn_w_in, ffn_w_out, s5_w_in, s5_lambda_re, s5_lambda_im, s5_log_step, s5_b_re, s5_b_im, s5_c_re, s5_c_im, s5_d, s5_w_glu, ml_w_in, ml_conv_w, ml_conv_b, ml_wq, ml_wk, ml_wv, ml_w_gates, ml_b_gates, ml_norm_g, ml_skip, ml_w_out)
    y_sample = _trunk(x_sample, norm_g, final_g, ffn_w_in, ffn_w_out, s5_w_in, s5_lambda_re, s5_lambda_im, s5_log_step, s5_b_re, s5_b_im, s5_c_re, s5_c_im, s5_d, s5_w_glu, ml_w_in, ml_conv_w, ml_conv_b, ml_wq, ml_wk, ml_wv, ml_w_gates, ml_b_gates, ml_norm_g, ml_skip, ml_w_out)
    return (y_prompt, y_sample)
```

```python
import functools

import jax
import jax.numpy as jnp
from jax import lax
from jax.experimental import pallas as pl
from jax.experimental.pallas import tpu as pltpu

F32 = jnp.float32
_MXU_DTYPE = jnp.bfloat16

D_MODEL = 2048
D_FF = 5632
EPS = 1e-6
S5_GROUP = 16
S5_GROUPS = D_MODEL // S5_GROUP
S5_STATE = 64
S5_CHUNK = 16
S5_ROW = S5_CHUNK * S5_GROUP
S5_ST2 = 2 * S5_STATE
ML_INNER = 2 * D_MODEL
ML_HEADS = 16
ML_DH = ML_INNER // ML_HEADS
ML_BLK = 4
ML_CONV = 5
ML_CHUNK = 128
ML_NGATE = 4 * ML_HEADS
HALO = 16

_VMEM_LIMIT = 52 * 1024 * 1024


def _cparams(sem):
    return pltpu.CompilerParams(dimension_semantics=sem, vmem_limit_bytes=_VMEM_LIMIT)


def _mm(a, b):
    return jnp.dot(a, b, preferred_element_type=F32)


def _mm_exact(a, b):
    return jnp.dot(a, b, preferred_element_type=F32, precision=lax.Precision.HIGHEST)


def _rms(x, g):
    ms = jnp.mean(x * x, axis=-1, keepdims=True)
    return x * lax.rsqrt(ms + EPS) * g


def _pick(n, pref):
    t = min(n, pref)
    while n % t:
        t //= 2
    return t


def _ffn_body(x_ref, g_ref, wg_ref, wu_ref, wo_ref, fg_ref, o_ref, hn_ref, *, n_j, final_norm):
    j = pl.program_id(1)

    @pl.when(j == 0)
    def _():
        hn_ref[...] = _rms(x_ref[...], g_ref[...]).astype(hn_ref.dtype)

    h = hn_ref[...]
    gate = _mm(h, wg_ref[...])
    up = _mm(h, wu_ref[...])
    act = (gate * jax.nn.sigmoid(gate) * up).astype(h.dtype)
    part = _mm(act, wo_ref[...])

    @pl.when(j == 0)
    def _():
        o_ref[...] = part

    @pl.when(j > 0)
    def _():
        o_ref[...] += part

    @pl.when(j == n_j - 1)
    def _():
        y = x_ref[...] + 0.5 * o_ref[...]
        if final_norm:
            y = _rms(y, fg_ref[...])
        o_ref[...] = y


def _ffn(x, g, w_in, w_out, final_g=None):
    t, d = x.shape
    tm = _pick(t, 512)
    tn = 512
    n_j = D_FF // tn
    fg = g if final_g is None else final_g
    body = functools.partial(_ffn_body, n_j=n_j, final_norm=final_g is not None)
    return pl.pallas_call(
        body,
        grid=(t // tm, n_j),
        in_specs=[
            pl.BlockSpec((tm, d), lambda i, j: (i, 0)),
            pl.BlockSpec((1, d), lambda i, j: (0, 0)),
            pl.BlockSpec((d, tn), lambda i, j: (0, j)),
            pl.BlockSpec((d, tn), lambda i, j: (0, j + n_j)),
            pl.BlockSpec((tn, d), lambda i, j: (j, 0)),
            pl.BlockSpec((1, d), lambda i, j: (0, 0)),
        ],
        out_specs=pl.BlockSpec((tm, d), lambda i, j: (i, 0)),
        out_shape=jax.ShapeDtypeStruct((t, d), F32),
        scratch_shapes=[pltpu.VMEM((tm, d), _MXU_DTYPE)],
        compiler_params=_cparams(("parallel", "arbitrary")),
        name="ffn",
    )(x, g.reshape(1, d), w_in, w_in, w_out, fg.reshape(1, d))


def _rms_mm_body(x_ref, g_ref, w_ref, o_ref, hn_ref):
    @pl.when(pl.program_id(1) == 0)
    def _():
        hn_ref[...] = _rms(x_ref[...], g_ref[...]).astype(hn_ref.dtype)

    o_ref[...] = _mm(hn_ref[...], w_ref[...]).astype(o_ref.dtype)


def _rms_mm(x, g, w, name):
    t, d = x.shape
    n = w.shape[1]
    tm = _pick(t, 512)
    tn = 512
    return pl.pallas_call(
        _rms_mm_body,
        grid=(t // tm, n // tn),
        in_specs=[
            pl.BlockSpec((tm, d), lambda i, j: (i, 0)),
            pl.BlockSpec((1, d), lambda i, j: (0, 0)),
            pl.BlockSpec((d, tn), lambda i, j: (0, j)),
        ],
        out_specs=pl.BlockSpec((tm, tn), lambda i, j: (i, j)),
        out_shape=jax.ShapeDtypeStruct((t, n), _MXU_DTYPE),
        scratch_shapes=[pltpu.VMEM((tm, d), _MXU_DTYPE)],
        compiler_params=_cparams(("parallel", "arbitrary")),
        name=name,
    )(x, g.reshape(1, d), w)


def _s5_kern_body(cq_ref, pd_ref, o_ref):
    for n in range(cq_ref.shape[0]):
        o_ref[n] = _mm_exact(cq_ref[n], pd_ref[n])


def _s5_tables(lam_re, lam_im, log_step, b_re, b_im, c_re, c_im, d):
    g_n, p_n, gs, ck = S5_GROUPS, S5_STATE, S5_GROUP, S5_CHUNK
    lam = lax.complex(jnp.minimum(lam_re.astype(F32), -1e-4), lam_im.astype(F32))
    delta = jnp.exp(log_step.astype(F32))[..., None]
    lam_bar = jnp.exp(lam * delta)
    b_bar = ((lam_bar - 1.0) / lam)[..., None] * lax.complex(b_re.astype(F32), b_im.astype(F32))
    c = lax.complex(c_re.astype(F32), c_im.astype(F32))
    pows = [jnp.ones_like(lam_bar)]
    for _ in range(ck):
        pows.append(pows[-1] * lam_bar)
    pw = jnp.stack(pows)

    x = pw[:ck, :, :, :, None] * b_bar[None]
    xt = x.transpose(1, 2, 3, 0, 4).reshape(2, g_n, p_n, ck * gs)
    pd_t = jnp.concatenate([jnp.real(xt), jnp.imag(xt)], axis=2).reshape(2 * g_n, 2 * p_n, ck * gs)
    cq_t = jnp.concatenate([jnp.real(c), -jnp.imag(c)], axis=-1).reshape(2 * g_n, gs, 2 * p_n)
    gb = 8
    kt = pl.pallas_call(
        _s5_kern_body,
        grid=(2 * g_n // gb,),
        in_specs=[
            pl.BlockSpec((gb, gs, 2 * p_n), lambda i: (i, 0, 0)),
            pl.BlockSpec((gb, 2 * p_n, ck * gs), lambda i: (i, 0, 0)),
        ],
        out_specs=pl.BlockSpec((gb, gs, ck * gs), lambda i: (i, 0, 0)),
        out_shape=jax.ShapeDtypeStruct((2 * g_n, gs, ck * gs), F32),
        compiler_params=_cparams(("parallel",)),
        name="s5_kernels",
    )(cq_t, pd_t)
    kt = kt.reshape(2, g_n, gs, ck, gs)
    s_i = jnp.arange(ck)[:, None]
    t_i = jnp.arange(ck)[None, :]
    dfw = t_i - s_i
    dbw = s_i - t_i
    tf = jnp.where((dfw >= 0)[None, None, :, :, None], kt[0][:, :, jnp.clip(dfw, 0, ck - 1), :], 0.0)
    tb = jnp.where((dbw >= 0)[None, None, :, :, None], kt[1][:, :, jnp.clip(dbw, 0, ck - 1), :], 0.0)
    toep = (tf + tb).transpose(0, 2, 4, 3, 1).reshape(g_n, ck * gs, ck * gs)

    def reim_last(z):
        return jnp.concatenate([jnp.real(z), jnp.imag(z)], axis=-1)

    xf = x[::-1, 0].transpose(1, 0, 3, 2).reshape(g_n, ck * gs, p_n)
    xb = x[:, 1].transpose(1, 0, 3, 2).reshape(g_n, ck * gs, p_n)
    w_state = jnp.concatenate([reim_last(xf), reim_last(xb)], axis=-1)

    zf = (c[0][None] * pw[1:ck + 1, 0][:, :, None, :]).transpose(1, 3, 0, 2).reshape(g_n, p_n, ck * gs)
    zb = (c[1][None] * pw[ck:0:-1, 1][:, :, None, :]).transpose(1, 3, 0, 2).reshape(g_n, p_n, ck * gs)
    wo_f = jnp.concatenate([jnp.real(zf), -jnp.imag(zf)], axis=1)
    wo_b = jnp.concatenate([jnp.real(zb), -jnp.imag(zb)], axis=1)

    def coef(a):
        return (jnp.concatenate([jnp.real(a), jnp.real(a)], -1), jnp.concatenate([-jnp.imag(a), jnp.imag(a)], -1))

    a_f = coef(pw[ck, 0])
    a_b = coef(pw[ck, 1])
    d_g = jnp.broadcast_to(d.astype(F32).reshape(g_n, 1, gs), (g_n, ck, gs)).reshape(g_n, 1, ck * gs)
    mx = _MXU_DTYPE
    return dict(toep=toep.astype(mx), w_state=w_state.astype(mx), wo_f=wo_f.astype(mx), wo_b=wo_b.astype(mx),
                a_f=a_f, a_b=a_b, d_g=d_g)


def _s5_state_in_body(u_ref, w_ref, vf_ref, vb_ref):
    v = _mm(u_ref[0], w_ref[0])
    vf_ref[...] = v[:, :S5_ST2]
    vb_ref[...] = v[:, S5_ST2:]


def _s5_scan_body(vf_ref, vb_ref, afr_ref, afi_ref, abr_ref, abi_ref, sf_ref, sb_ref, cf_ref, cb_ref, *, rt):
    @pl.when(pl.program_id(1) == 0)
    def _():
        cf_ref[...] = jnp.zeros_like(cf_ref)
        cb_ref[...] = jnp.zeros_like(cb_ref)

    afr, afi, abr, abi = afr_ref[...], afi_ref[...], abr_ref[...], abi_ref[...]

    def step(k, carry):
        s_f, s_b = carry
        sf_ref[k, 0] = s_f.astype(sf_ref.dtype)
        s_f = afr * s_f + afi * pltpu.roll(s_f, S5_STATE, 1) + vf_ref[k, 0]
        kb = rt - 1 - k
        sb_ref[kb, 0] = s_b.astype(sb_ref.dtype)
        s_b = abr * s_b + abi * pltpu.roll(s_b, S5_STATE, 1) + vb_ref[kb, 0]
        return s_f, s_b

    s_f, s_b = lax.fori_loop(0, rt, step, (cf_ref[...], cb_ref[...]))
    cf_ref[...] = s_f
    cb_ref[...] = s_b


def _s5_out_body(u_ref, t_ref, sf_ref, sb_ref, wof_ref, wob_ref, d_ref, o_ref):
    u = u_ref[0]
    y = _mm(u, t_ref[0]) + _mm(sf_ref[...], wof_ref[0]) + _mm(sb_ref[...], wob_ref[0])
    y = y + d_ref[0] * u.astype(F32)
    o_ref[0] = jax.nn.gelu(y).astype(o_ref.dtype)


def _glu_body(y_ref, wv_ref, wg_ref, x_ref, o_ref):
    y = y_ref[...]
    val = _mm(y, wv_ref[...])
    gate = _mm(y, wg_ref[...])
    o_ref[...] = x_ref[...] + val * jax.nn.sigmoid(gate)


def _s5_mixer(x, bsz, length, g, w_in, tabs, w_glu):
    t, d = x.shape
    g_n, ck = S5_GROUPS, S5_CHUNK
    r_n = length // ck
    rows = r_n * bsz
    mx = _MXU_DTYPE
    u = _rms_mm(x, g, w_in, "s5_in")
    ug = u.reshape(bsz, r_n, ck, g_n, S5_GROUP).transpose(3, 1, 0, 2, 4).reshape(g_n, rows, S5_ROW)

    vf, vb = pl.pallas_call(
        _s5_state_in_body,
        grid=(g_n,),
        in_specs=[
            pl.BlockSpec((1, rows, S5_ROW), lambda i: (i, 0, 0)),
            pl.BlockSpec((1, S5_ROW, 2 * S5_ST2), lambda i: (i, 0, 0)),
        ],
        out_specs=[pl.BlockSpec((rows, S5_ST2), lambda i: (0, i))] * 2,
        out_shape=[jax.ShapeDtypeStruct((rows, g_n * S5_ST2), F32)] * 2,
        compiler_params=_cparams(("parallel",)),
        name="s5_state_in",
    )(ug, tabs["w_state"])

    rt = _pick(r_n, 16)
    n_rt = r_n // rt
    v4 = (r_n, bsz, g_n, S5_ST2)
    blk = (rt, 1, g_n, S5_ST2)
    fwd = lambda b, i: (i, b, 0, 0)
    bwd = lambda b, i: (n_rt - 1 - i, b, 0, 0)
    coef_spec = pl.BlockSpec((g_n, S5_ST2), lambda b, i: (0, 0))
    sf, sb = pl.pallas_call(
        functools.partial(_s5_scan_body, rt=rt),
        grid=(bsz, n_rt),
        in_specs=[pl.BlockSpec(blk, fwd), pl.BlockSpec(blk, bwd), coef_spec, coef_spec, coef_spec, coef_spec],
        out_specs=[pl.BlockSpec(blk, fwd), pl.BlockSpec(blk, bwd)],
        out_shape=[jax.ShapeDtypeStruct(v4, mx)] * 2,
        scratch_shapes=[pltpu.VMEM((g_n, S5_ST2), F32)] * 2,
        compiler_params=_cparams(("parallel", "arbitrary")),
        name="s5_scan",
    )(vf.reshape(v4), vb.reshape(v4), tabs["a_f"][0], tabs["a_f"][1], tabs["a_b"][0], tabs["a_b"][1])
    sf = sf.reshape(rows, g_n * S5_ST2)
    sb = sb.reshape(rows, g_n * S5_ST2)

    yg = pl.pallas_call(
        _s5_out_body,
        grid=(g_n,),
        in_specs=[
            pl.BlockSpec((1, rows, S5_ROW), lambda i: (i, 0, 0)),
            pl.BlockSpec((1, S5_ROW, S5_ROW), lambda i: (i, 0, 0)),
            pl.BlockSpec((rows, S5_ST2), lambda i: (0, i)),
            pl.BlockSpec((rows, S5_ST2), lambda i: (0, i)),
            pl.BlockSpec((1, S5_ST2, S5_ROW), lambda i: (i, 0, 0)),
            pl.BlockSpec((1, S5_ST2, S5_ROW), lambda i: (i, 0, 0)),
            pl.BlockSpec((1, 1, S5_ROW), lambda i: (i, 0, 0)),
        ],
        out_specs=pl.BlockSpec((1, rows, S5_ROW), lambda i: (i, 0, 0)),
        out_shape=jax.ShapeDtypeStruct((g_n, rows, S5_ROW), mx),
        compiler_params=_cparams(("parallel",)),
        name="s5_out",
    )(ug, tabs["toep"], sf, sb, tabs["wo_f"], tabs["wo_b"], tabs["d_g"])
    y = yg.reshape(g_n, r_n, bsz, ck, S5_GROUP).transpose(2, 1, 3, 0, 4).reshape(t, d)

    tm = _pick(t, 512)
    tn = 512
    n_j = d // tn
    return pl.pallas_call(
        _glu_body,
        grid=(t // tm, n_j),
        in_specs=[
            pl.BlockSpec((tm, d), lambda i, j: (i, 0)),
            pl.BlockSpec((d, tn), lambda i, j: (0, j)),
            pl.BlockSpec((d, tn), lambda i, j: (0, j + n_j)),
            pl.BlockSpec((tm, tn), lambda i, j: (i, j)),
        ],
        out_specs=pl.BlockSpec((tm, tn), lambda i, j: (i, j)),
        out_shape=jax.ShapeDtypeStruct((t, d), F32),
        compiler_params=_cparams(("parallel", "arbitrary")),
        name="s5_glu",
    )(y, w_glu, w_glu, x)


def _log_sigmoid(x):
    return jnp.minimum(x, 0.0) - jnp.log(1.0 + jnp.exp(-jnp.abs(x)))


def _ml_pre_body(xm_ref, prev_ref, next_ref, cw_ref, cb_ref, bdq_ref, bdk_ref, bdv_ref, wg_ref, wgt_ref, bg_ref,
                 bgt_ref, q_ref, k_ref, v_ref, xc_ref, gates_ref, p_ref, s_ref, gatest_ref, pt_ref, st_ref,
                 ext_ref, acc_ref, acct_ref, *, tiles_per_seq):
    i = pl.program_id(0)
    h = pl.program_id(1)
    tm = xm_ref.shape[0]
    pos = lax.rem(i, tiles_per_seq)
    xm_b = xm_ref[...]
    ext_ref[pl.ds(HALO, tm), :] = xm_b.astype(F32)
    ext_ref[pl.ds(0, HALO), :] = jnp.where(pos == 0, 0.0, prev_ref[...].astype(F32))
    ext_ref[pl.ds(HALO + tm, HALO), :] = jnp.where(pos == tiles_per_seq - 1, 0.0, next_ref[...].astype(F32))
    pre = cb_ref[...]
    for jj in range(ML_CONV):
        pre = pre + cw_ref[jj:jj + 1, :] * ext_ref[pl.ds(HALO - ML_CONV // 2 + jj, tm), :]
    xc = (pre * jax.nn.sigmoid(pre)).astype(xm_b.dtype)
    xc_ref[...] = xc
    qh = _mm(xc, bdq_ref[0]).astype(xm_b.dtype)
    kh = _mm(xc, bdk_ref[0]).astype(xm_b.dtype)
    vh = _mm(xm_b, bdv_ref[0]).astype(xm_b.dtype)
    q_ref[...] = qh
    k_ref[...] = (kh.astype(F32) * (ML_DH ** -0.5)).astype(xm_b.dtype)
    v_ref[...] = vh
    nt = (((1,), (1,)), ((), ()))
    part = _mm(qh, wg_ref[0]) + _mm(kh, wg_ref[1]) + _mm(vh, wg_ref[2])
    part_t = (lax.dot_general(wgt_ref[0], qh, nt, preferred_element_type=F32)
              + lax.dot_general(wgt_ref[1], kh, nt, preferred_element_type=F32)
              + lax.dot_general(wgt_ref[2], vh, nt, preferred_element_type=F32))

    @pl.when(h == 0)
    def _():
        acc_ref[...] = part + bg_ref[...]
        acct_ref[...] = part_t + bgt_ref[...]

    @pl.when(h > 0)
    def _():
        acc_ref[...] += part
        acct_ref[...] += part_t

    @pl.when(h == ML_HEADS - 1)
    def _():
        r_i = lax.broadcasted_iota(jnp.int32, (ML_CHUNK, ML_CHUNK), 0)
        c_i = lax.broadcasted_iota(jnp.int32, (ML_CHUNK, ML_CHUNK), 1)
        lower = (c_i <= r_i).astype(F32)
        upper = (c_i >= r_i).astype(F32)
        gates = acc_ref[...]
        gates_t = acct_ref[...]
        gates_ref[...] = gates
        gatest_ref[...] = gates_t
        for c in range(tm // ML_CHUNK):
            sl = pl.ds(c * ML_CHUNK, ML_CHUNK)
            ls = _log_sigmoid(gates[c * ML_CHUNK:(c + 1) * ML_CHUNK, :])
            p_ref[sl, :] = _mm_exact(lower, ls)
            s_ref[sl, :] = _mm_exact(upper, ls)
            lst = _log_sigmoid(gates_t[:, c * ML_CHUNK:(c + 1) * ML_CHUNK])
            pt_ref[:, sl] = _mm_exact(lst, upper)
            st_ref[:, sl] = _mm_exact(lst, lower)


def _mlstm_dir(d, q_ref, k_ref, v_ref, col_ref, row_ref, out_ref, c_ref, n_ref, m_ref):
    q = q_ref[...]
    k = k_ref[...]
    v = v_ref[...]
    col = col_ref[0]
    row = row_ref[0]
    bc = col[:, d:d + 1]
    lic = col[:, 2 + d:3 + d]
    bcr = row[d:d + 1, :]
    lir = row[2 + d:3 + d, :]
    t_i = lax.broadcasted_iota(jnp.int32, (ML_CHUNK, ML_CHUNK), 0)
    s_i = lax.broadcasted_iota(jnp.int32, (ML_CHUNK, ML_CHUNK), 1)
    if d == 0:
        mask = s_i <= t_i
        g = bcr[:, ML_CHUNK - 1:ML_CHUNK]
    else:
        mask = s_i >= t_i
        g = bcr[:, 0:1]
    m_prev = m_ref[d]
    a = bc + m_prev
    drow = lir - bcr
    dm = jnp.where(mask, bc + drow, -jnp.inf)
    m_t = jnp.maximum(a, jnp.max(dm, axis=1, keepdims=True))
    dec = jnp.exp(dm - m_t)
    s = lax.dot_general(q, k, (((1,), (1,)), ((), ())), preferred_element_type=F32) * dec
    e = jnp.exp(a - m_t)
    c_t = c_ref[d]
    n_v = n_ref[d]
    num = e * _mm(q, c_t.astype(q.dtype)) + _mm(s.astype(q.dtype), v)
    qn = jnp.sum(q.astype(F32) * n_v, axis=1, keepdims=True)
    den = e * qn + jnp.sum(s, axis=1, keepdims=True)
    out_ref[...] = num / jnp.maximum(jnp.abs(den), jnp.exp(-m_t))
    r_col = g - bc + lic
    r_row = g + drow
    m_new = jnp.maximum(g + m_prev, jnp.max(r_row, axis=1, keepdims=True))
    decay = jnp.exp(g + m_prev - m_new)
    wr = jnp.exp(r_col - m_new)
    vw = (v.astype(F32) * wr).astype(q.dtype)
    c_ref[d] = decay * c_t + lax.dot_general(k, vw, (((0,), (0,)), ((), ())), preferred_element_type=F32)
    n_ref[d] = decay * n_v + jnp.sum(k.astype(F32) * wr, axis=0, keepdims=True)
    m_ref[d] = m_new


def _mlstm_body(qf, kf, vf, colf, rowf, qb, kb, vb, colb, rowb, hf_ref, hb_ref, c_ref, n_ref, m_ref):
    @pl.when(pl.program_id(2) == 0)
    def _():
        c_ref[...] = jnp.zeros_like(c_ref)
        n_ref[...] = jnp.zeros_like(n_ref)
        m_ref[...] = jnp.zeros_like(m_ref)

    _mlstm_dir(0, qf, kf, vf, colf, rowf, hf_ref, c_ref, n_ref, m_ref)
    _mlstm_dir(1, qb, kb, vb, colb, rowb, hb_ref, c_ref, n_ref, m_ref)


def _ml_out_body(hf_ref, hb_ref, z_ref, xc_ref, ng_ref, sk_ref, w_ref, x_ref, o_ref, act_ref):
    @pl.when(pl.program_id(1) == 0)
    def _():
        for hd in range(ML_HEADS):
            sl = pl.ds(hd * ML_DH, ML_DH)
            hh = hf_ref[:, sl] + hb_ref[:, sl]
            mu = jnp.mean(hh, axis=-1, keepdims=True)
            cen = hh - mu
            var = jnp.mean(cen * cen, axis=-1, keepdims=True)
            hn = cen * lax.rsqrt(var + EPS) * ng_ref[:, sl]
            out = jax.nn.sigmoid(z_ref[:, sl].astype(F32)) * (hn + sk_ref[:, sl] * xc_ref[:, sl].astype(F32))
            act_ref[:, sl] = out.astype(act_ref.dtype)

    o_ref[...] = x_ref[...] + _mm(act_ref[...], w_ref[...])


def _blockdiag(w):
    nb = ML_DH // ML_BLK
    wt = w.astype(F32).reshape(ML_HEADS, nb, ML_BLK, ML_BLK)
    eye = jnp.eye(nb, dtype=F32)
    return jnp.einsum("hncd,nm->hncmd", wt, eye).reshape(ML_HEADS, ML_DH, ML_DH).astype(_MXU_DTYPE)


def _mlstm_mixer(x, bsz, length, g, w_in, conv_w, conv_b, wq, wk, wv, w_gates, b_gates, norm_g, skip, w_out):
    t, d = x.shape
    mx = _MXU_DTYPE
    di, dh, nh = ML_INNER, ML_DH, ML_HEADS
    xz = _rms_mm(x, g, w_in, "ml_in")

    tm = _pick(length, 512)
    tiles_per_seq = length // tm
    hb = tm // HALO
    n_h = t // HALO
    wg = w_gates.astype(mx)
    wgt = wg.transpose(0, 2, 1)
    col_spec = lambda: pl.BlockSpec((tm, dh), lambda i, h: (i, h))
    small = lambda: pl.BlockSpec((tm, ML_NGATE), lambda i, h: (i, 0))
    small_t = lambda: pl.BlockSpec((ML_NGATE, tm), lambda i, h: (0, i))
    bd_spec = lambda: pl.BlockSpec((1, dh, dh), lambda i, h: (h, 0, 0))
    outs = pl.pallas_call(
        functools.partial(_ml_pre_body, tiles_per_seq=tiles_per_seq),
        grid=(t // tm, nh),
        in_specs=[
            col_spec(),
            pl.BlockSpec((HALO, dh), lambda i, h: (jnp.maximum(i * hb - 1, 0), h)),
            pl.BlockSpec((HALO, dh), lambda i, h: (jnp.minimum((i + 1) * hb, n_h - 1), h)),
            pl.BlockSpec((ML_CONV, dh), lambda i, h: (0, h)),
            pl.BlockSpec((1, dh), lambda i, h: (0, h)),
            bd_spec(), bd_spec(), bd_spec(),
            pl.BlockSpec((3, dh, ML_NGATE), lambda i, h: (0, h, 0)),
            pl.BlockSpec((3, ML_NGATE, dh), lambda i, h: (0, 0, h)),
            pl.BlockSpec((1, ML_NGATE), lambda i, h: (0, 0)),
            pl.BlockSpec((ML_NGATE, 1), lambda i, h: (0, 0)),
        ],
        out_specs=[col_spec(), col_spec(), col_spec(), col_spec(), small(), small(), small(),
                   small_t(), small_t(), small_t()],
        out_shape=[jax.ShapeDtypeStruct((t, di), mx)] * 4 + [jax.ShapeDtypeStruct((t, ML_NGATE), F32)] * 3
        + [jax.ShapeDtypeStruct((ML_NGATE, t), F32)] * 3,
        scratch_shapes=[pltpu.VMEM((tm + 2 * HALO, dh), F32), pltpu.VMEM((tm, ML_NGATE), F32),
                        pltpu.VMEM((ML_NGATE, tm), F32)],
        compiler_params=_cparams(("parallel", "arbitrary")),
        name="ml_pre",
    )(xz, xz, xz, conv_w.astype(F32), conv_b.astype(F32).reshape(1, di), _blockdiag(wq), _blockdiag(wk),
      _blockdiag(wv), wg, wgt, b_gates.astype(F32).reshape(1, ML_NGATE), b_gates.astype(F32).reshape(ML_NGATE, 1))
    q, k, v, xc, gates, pre, suf, gates_t, pre_t, suf_t = outs
    cols = jnp.stack([pre[:, nh:2 * nh], suf[:, 3 * nh:], gates[:, :nh], gates[:, 2 * nh:3 * nh]], axis=-1)
    cols = cols.transpose(1, 0, 2)
    rows = jnp.stack([pre_t[nh:2 * nh], suf_t[3 * nh:], gates_t[:nh], gates_t[2 * nh:3 * nh]], axis=1)

    nc = length // ML_CHUNK
    fw = lambda b, h, j: (b * nc + j, h)
    bw = lambda b, h, j: (b * nc + nc - 1 - j, h)
    qkv = lambda f: pl.BlockSpec((ML_CHUNK, dh), f)
    colspec = lambda f: pl.BlockSpec((1, ML_CHUNK, 4), lambda b, h, j: (h, f(b, h, j)[0], 0))
    rowspec = lambda f: pl.BlockSpec((1, 4, ML_CHUNK), lambda b, h, j: (h, 0, f(b, h, j)[0]))
    hf, hbk = pl.pallas_call(
        _mlstm_body,
        grid=(bsz, nh, nc),
        in_specs=[qkv(fw), qkv(fw), qkv(fw), colspec(fw), rowspec(fw),
                  qkv(bw), qkv(bw), qkv(bw), colspec(bw), rowspec(bw)],
        out_specs=[qkv(fw), qkv(bw)],
        out_shape=[jax.ShapeDtypeStruct((t, di), F32)] * 2,
        scratch_shapes=[pltpu.VMEM((2, dh, dh), F32), pltpu.VMEM((2, 1, dh), F32), pltpu.VMEM((2, 1, 1), F32)],
        compiler_params=_cparams(("parallel", "parallel", "arbitrary")),
        name="mlstm",
    )(q, k, v, cols, rows, q, k, v, cols, rows)

    tmo = _pick(t, 256)
    tn = 512
    return pl.pallas_call(
        _ml_out_body,
        grid=(t // tmo, d // tn),
        in_specs=[
            pl.BlockSpec((tmo, di), lambda i, j: (i, 0)),
            pl.BlockSpec((tmo, di), lambda i, j: (i, 0)),
            pl.BlockSpec((tmo, di), lambda i, j: (i, 1)),
            pl.BlockSpec((tmo, di), lambda i, j: (i, 0)),
            pl.BlockSpec((1, di), lambda i, j: (0, 0)),
            pl.BlockSpec((1, di), lambda i, j: (0, 0)),
            pl.BlockSpec((di, tn), lambda i, j: (0, j)),
            pl.BlockSpec((tmo, tn), lambda i, j: (i, j)),
        ],
        out_specs=pl.BlockSpec((tmo, tn), lambda i, j: (i, j)),
        out_shape=jax.ShapeDtypeStruct((t, d), F32),
        scratch_shapes=[pltpu.VMEM((tmo, di), mx)],
        compiler_params=_cparams(("parallel", "arbitrary")),
        name="ml_out",
    )(hf, hbk, xz, xc, norm_g.astype(F32).reshape(1, di), skip.astype(F32).reshape(1, di), w_out, x)


def _trunk(x3, p):
    bsz, length, d = x3.shape
    x = x3.reshape(bsz * length, d)
    x = _ffn(x, p["norm_g"][0, 0], p["ffn_w_in"][0, 0], p["ffn_w_out"][0, 0])
    x = _s5_mixer(x, bsz, length, p["norm_g"][0, 1], p["s5_w_in"], p["s5_tabs"], p["s5_w_glu"])
    x = _ffn(x, p["norm_g"][0, 2], p["ffn_w_in"][0, 1], p["ffn_w_out"][0, 1])
    x = _ffn(x, p["norm_g"][1, 0], p["ffn_w_in"][1, 0], p["ffn_w_out"][1, 0])
    x = _mlstm_mixer(x, bsz, length, p["norm_g"][1, 1], p["ml_w_in"], *p["ml_rest"])
    x = _ffn(x, p["norm_g"][1, 2], p["ffn_w_in"][1, 1], p["ffn_w_out"][1, 1], final_g=p["final_g"])
    return x.reshape(bsz, length, d)


def kernel(x_prompt, x_sample, norm_g, final_g, ffn_w_in, ffn_w_out, s5_w_in, s5_lambda_re, s5_lambda_im, s5_log_step, s5_b_re, s5_b_im, s5_c_re, s5_c_im, s5_d, s5_w_glu, ml_w_in, ml_conv_w, ml_conv_b, ml_wq, ml_wk, ml_wv, ml_w_gates, ml_b_gates, ml_norm_g, ml_skip, ml_w_out):
    mx = _MXU_DTYPE
    p = dict(
        norm_g=norm_g.astype(F32), final_g=final_g.astype(F32),
        ffn_w_in=ffn_w_in.astype(mx), ffn_w_out=ffn_w_out.astype(mx),
        s5_w_in=s5_w_in[0].astype(mx), s5_w_glu=s5_w_glu[0].astype(mx),
        s5_tabs=_s5_tables(s5_lambda_re[0], s5_lambda_im[0], s5_log_step[0], s5_b_re[0], s5_b_im[0], s5_c_re[0],
                           s5_c_im[0], s5_d[0]),
        ml_w_in=ml_w_in[0].astype(mx),
        ml_rest=(ml_conv_w[0], ml_conv_b[0], ml_wq[0], ml_wk[0], ml_wv[0], ml_w_gates[0], ml_b_gates[0],
                 ml_norm_g[0], ml_skip[0], ml_w_out[0].astype(mx)),
    )
    return (_trunk(x_prompt, p), _trunk(x_sample, p))
```

```python
import functools

import jax
import jax.numpy as jnp
from jax import lax
from jax.experimental import pallas as pl
from jax.experimental.pallas import tpu as pltpu

F32 = jnp.float32
_MXU_DTYPE = jnp.bfloat16

D_MODEL = 2048
D_FF = 5632
EPS = 1e-6
S5_GROUP = 16
S5_GROUPS = D_MODEL // S5_GROUP
S5_STATE = 64
S5_CHUNK = 16
S5_ROW = S5_CHUNK * S5_GROUP
S5_ST2 = 2 * S5_STATE
ML_INNER = 2 * D_MODEL
ML_HEADS = 16
ML_DH = ML_INNER // ML_HEADS
ML_BLK = 4
ML_CONV = 5
ML_CHUNK = 128
ML_NGATE = 4 * ML_HEADS
ML_HP = 4
ML_AUG = 128
HALO = 16

_VMEM_LIMIT = 52 * 1024 * 1024


def _cparams(sem):
    return pltpu.CompilerParams(dimension_semantics=sem, vmem_limit_bytes=_VMEM_LIMIT)


def _mm(a, b):
    return jnp.dot(a, b, preferred_element_type=F32)


def _mm_exact(a, b):
    return jnp.dot(a, b, preferred_element_type=F32, precision=lax.Precision.HIGHEST)


def _rms(x, g):
    ms = jnp.mean(x * x, axis=-1, keepdims=True)
    return x * lax.rsqrt(ms + EPS) * g


def _pick(n, pref):
    t = min(n, pref)
    while n % t:
        t //= 2
    return t


def _ffn_body(x_ref, g_ref, wg_ref, wu_ref, wo_ref, fg_ref, o_ref, hn_ref, *, n_j, final_norm):
    j = pl.program_id(1)

    @pl.when(j == 0)
    def _():
        hn_ref[...] = _rms(x_ref[...], g_ref[...]).astype(hn_ref.dtype)
        o_ref[...] = jnp.zeros_like(o_ref)

    h = hn_ref[...]
    gate = _mm(h, wg_ref[...])
    up = _mm(h, wu_ref[...])
    act = (gate * jax.nn.sigmoid(gate) * up).astype(h.dtype)
    o_ref[...] += _mm(act, wo_ref[...])

    @pl.when(j == n_j - 1)
    def _():
        y = x_ref[...] + 0.5 * o_ref[...]
        if final_norm:
            y = _rms(y, fg_ref[...])
        o_ref[...] = y


def _ffn(x, g, w_in, w_out, final_g=None):
    t, d = x.shape
    tm = _pick(t, 512)
    tn = 512
    n_j = D_FF // tn
    fg = g if final_g is None else final_g
    body = functools.partial(_ffn_body, n_j=n_j, final_norm=final_g is not None)
    return pl.pallas_call(
        body,
        grid=(t // tm, n_j),
        in_specs=[
            pl.BlockSpec((tm, d), lambda i, j: (i, 0)),
            pl.BlockSpec((1, d), lambda i, j: (0, 0)),
            pl.BlockSpec((d, tn), lambda i, j: (0, j)),
            pl.BlockSpec((d, tn), lambda i, j: (0, j + n_j)),
            pl.BlockSpec((tn, d), lambda i, j: (j, 0)),
            pl.BlockSpec((1, d), lambda i, j: (0, 0)),
        ],
        out_specs=pl.BlockSpec((tm, d), lambda i, j: (i, 0)),
        out_shape=jax.ShapeDtypeStruct((t, d), F32),
        scratch_shapes=[pltpu.VMEM((tm, d), _MXU_DTYPE)],
        compiler_params=_cparams(("parallel", "arbitrary")),
        name="ffn",
    )(x, g.reshape(1, d), w_in, w_in, w_out, fg.reshape(1, d))


def _rms_mm_body(x_ref, g_ref, w_ref, o_ref, hn_ref):
    @pl.when(pl.program_id(1) == 0)
    def _():
        hn_ref[...] = _rms(x_ref[...], g_ref[...]).astype(hn_ref.dtype)

    o_ref[...] = _mm(hn_ref[...], w_ref[...]).astype(o_ref.dtype)


def _rms_mm(x, g, w, name):
    t, d = x.shape
    n = w.shape[1]
    tm = _pick(t, 512)
    tn = 512
    return pl.pallas_call(
        _rms_mm_body,
        grid=(t // tm, n // tn),
        in_specs=[
            pl.BlockSpec((tm, d), lambda i, j: (i, 0)),
            pl.BlockSpec((1, d), lambda i, j: (0, 0)),
            pl.BlockSpec((d, tn), lambda i, j: (0, j)),
        ],
        out_specs=pl.BlockSpec((tm, tn), lambda i, j: (i, j)),
        out_shape=jax.ShapeDtypeStruct((t, n), _MXU_DTYPE),
        scratch_shapes=[pltpu.VMEM((tm, d), _MXU_DTYPE)],
        compiler_params=_cparams(("parallel", "arbitrary")),
        name=name,
    )(x, g.reshape(1, d), w)


def _s5_kern_body(cq_ref, pd_ref, o_ref):
    for n in range(cq_ref.shape[0]):
        o_ref[n] = _mm_exact(cq_ref[n], pd_ref[n])


def _s5_tables(lam_re, lam_im, log_step, b_re, b_im, c_re, c_im, d):
    g_n, p_n, gs, ck = S5_GROUPS, S5_STATE, S5_GROUP, S5_CHUNK
    lam = lax.complex(jnp.minimum(lam_re.astype(F32), -1e-4), lam_im.astype(F32))
    delta = jnp.exp(log_step.astype(F32))[..., None]
    lam_bar = jnp.exp(lam * delta)
    b_bar = ((lam_bar - 1.0) / lam)[..., None] * lax.complex(b_re.astype(F32), b_im.astype(F32))
    c = lax.complex(c_re.astype(F32), c_im.astype(F32))
    pows = [jnp.ones_like(lam_bar)]
    for _ in range(ck):
        pows.append(pows[-1] * lam_bar)
    pw = jnp.stack(pows)

    x = pw[:ck, :, :, :, None] * b_bar[None]
    xt = x.transpose(1, 2, 3, 0, 4).reshape(2, g_n, p_n, ck * gs)
    pd_t = jnp.concatenate([jnp.real(xt), jnp.imag(xt)], axis=2).reshape(2 * g_n, 2 * p_n, ck * gs)
    cq_t = jnp.concatenate([jnp.real(c), -jnp.imag(c)], axis=-1).reshape(2 * g_n, gs, 2 * p_n)
    gb = 8
    kt = pl.pallas_call(
        _s5_kern_body,
        grid=(2 * g_n // gb,),
        in_specs=[
            pl.BlockSpec((gb, gs, 2 * p_n), lambda i: (i, 0, 0)),
            pl.BlockSpec((gb, 2 * p_n, ck * gs), lambda i: (i, 0, 0)),
        ],
        out_specs=pl.BlockSpec((gb, gs, ck * gs), lambda i: (i, 0, 0)),
        out_shape=jax.ShapeDtypeStruct((2 * g_n, gs, ck * gs), F32),
        compiler_params=_cparams(("parallel",)),
        name="s5_kernels",
    )(cq_t, pd_t)
    kt = kt.reshape(2, g_n, gs, ck, gs)
    s_i = jnp.arange(ck)[:, None]
    t_i = jnp.arange(ck)[None, :]
    dfw = t_i - s_i
    dbw = s_i - t_i
    tf = jnp.where((dfw >= 0)[None, None, :, :, None], kt[0][:, :, jnp.clip(dfw, 0, ck - 1), :], 0.0)
    tb = jnp.where((dbw >= 0)[None, None, :, :, None], kt[1][:, :, jnp.clip(dbw, 0, ck - 1), :], 0.0)
    toep = (tf + tb).transpose(0, 2, 4, 3, 1).reshape(g_n, ck * gs, ck * gs)

    def reim_last(z):
        return jnp.concatenate([jnp.real(z), jnp.imag(z)], axis=-1)

    xf = x[::-1, 0].transpose(1, 0, 3, 2).reshape(g_n, ck * gs, p_n)
    xb = x[:, 1].transpose(1, 0, 3, 2).reshape(g_n, ck * gs, p_n)
    w_state = jnp.concatenate([reim_last(xf), reim_last(xb)], axis=-1)

    zf = (c[0][None] * pw[1:ck + 1, 0][:, :, None, :]).transpose(1, 3, 0, 2).reshape(g_n, p_n, ck * gs)
    zb = (c[1][None] * pw[ck:0:-1, 1][:, :, None, :]).transpose(1, 3, 0, 2).reshape(g_n, p_n, ck * gs)
    wo_f = jnp.concatenate([jnp.real(zf), -jnp.imag(zf)], axis=1)
    wo_b = jnp.concatenate([jnp.real(zb), -jnp.imag(zb)], axis=1)

    def coef(a):
        return (jnp.concatenate([jnp.real(a), jnp.real(a)], -1), jnp.concatenate([-jnp.imag(a), jnp.imag(a)], -1))

    a_f = coef(pw[ck, 0])
    a_b = coef(pw[ck, 1])
    d_g = jnp.broadcast_to(d.astype(F32).reshape(g_n, 1, gs), (g_n, ck, gs)).reshape(g_n, 1, ck * gs)
    mx = _MXU_DTYPE
    return dict(toep=toep.astype(mx), w_state=w_state.astype(mx), wo_f=wo_f.astype(mx), wo_b=wo_b.astype(mx),
                a_f=a_f, a_b=a_b, d_g=d_g)


def _s5_state_in_body(u_ref, w_ref, vf_ref, vb_ref):
    v = _mm(u_ref[0], w_ref[0])
    vf_ref[...] = v[:, :S5_ST2]
    vb_ref[...] = v[:, S5_ST2:]


def _s5_scan_body(vf_ref, vb_ref, afr_ref, afi_ref, abr_ref, abi_ref, sf_ref, sb_ref, cf_ref, cb_ref, *, rt):
    @pl.when(pl.program_id(1) == 0)
    def _():
        cf_ref[...] = jnp.zeros_like(cf_ref)
        cb_ref[...] = jnp.zeros_like(cb_ref)

    afr, afi, abr, abi = afr_ref[...], afi_ref[...], abr_ref[...], abi_ref[...]

    def step(k, carry):
        s_f, s_b = carry
        sf_ref[k, 0] = s_f.astype(sf_ref.dtype)
        s_f = afr * s_f + afi * pltpu.roll(s_f, S5_STATE, 1) + vf_ref[k, 0]
        kb = rt - 1 - k
        sb_ref[kb, 0] = s_b.astype(sb_ref.dtype)
        s_b = abr * s_b + abi * pltpu.roll(s_b, S5_STATE, 1) + vb_ref[kb, 0]
        return s_f, s_b

    s_f, s_b = lax.fori_loop(0, rt, step, (cf_ref[...], cb_ref[...]))
    cf_ref[...] = s_f
    cb_ref[...] = s_b


def _s5_out_body(u_ref, t_ref, sf_ref, sb_ref, wof_ref, wob_ref, d_ref, o_ref):
    u = u_ref[0]
    y = _mm(u, t_ref[0]) + _mm(sf_ref[...], wof_ref[0]) + _mm(sb_ref[...], wob_ref[0])
    y = y + d_ref[0] * u.astype(F32)
    o_ref[0] = jax.nn.gelu(y).astype(o_ref.dtype)


def _glu_body(y_ref, wv_ref, wg_ref, x_ref, o_ref):
    y = y_ref[...]
    val = _mm(y, wv_ref[...])
    gate = _mm(y, wg_ref[...])
    o_ref[...] = x_ref[...] + val * jax.nn.sigmoid(gate)


def _s5_mixer(x, bsz, length, g, w_in, tabs, w_glu):
    t, d = x.shape
    g_n, ck = S5_GROUPS, S5_CHUNK
    r_n = length // ck
    rows = r_n * bsz
    mx = _MXU_DTYPE
    u = _rms_mm(x, g, w_in, "s5_in")
    ug = u.reshape(bsz, r_n, ck, g_n, S5_GROUP).transpose(3, 1, 0, 2, 4).reshape(g_n, rows, S5_ROW)

    vf, vb = pl.pallas_call(
        _s5_state_in_body,
        grid=(g_n,),
        in_specs=[
            pl.BlockSpec((1, rows, S5_ROW), lambda i: (i, 0, 0)),
            pl.BlockSpec((1, S5_ROW, 2 * S5_ST2), lambda i: (i, 0, 0)),
        ],
        out_specs=[pl.BlockSpec((rows, S5_ST2), lambda i: (0, i))] * 2,
        out_shape=[jax.ShapeDtypeStruct((rows, g_n * S5_ST2), F32)] * 2,
        compiler_params=_cparams(("parallel",)),
        name="s5_state_in",
    )(ug, tabs["w_state"])

    rt = _pick(r_n, 16)
    n_rt = r_n // rt
    v4 = (r_n, bsz, g_n, S5_ST2)
    blk = (rt, 1, g_n, S5_ST2)
    fwd = lambda b, i: (i, b, 0, 0)
    bwd = lambda b, i: (n_rt - 1 - i, b, 0, 0)
    coef_spec = pl.BlockSpec((g_n, S5_ST2), lambda b, i: (0, 0))
    sf, sb = pl.pallas_call(
        functools.partial(_s5_scan_body, rt=rt),
        grid=(bsz, n_rt),
        in_specs=[pl.BlockSpec(blk, fwd), pl.BlockSpec(blk, bwd), coef_spec, coef_spec, coef_spec, coef_spec],
        out_specs=[pl.BlockSpec(blk, fwd), pl.BlockSpec(blk, bwd)],
        out_shape=[jax.ShapeDtypeStruct(v4, mx)] * 2,
        scratch_shapes=[pltpu.VMEM((g_n, S5_ST2), F32)] * 2,
        compiler_params=_cparams(("parallel", "arbitrary")),
        name="s5_scan",
    )(vf.reshape(v4), vb.reshape(v4), tabs["a_f"][0], tabs["a_f"][1], tabs["a_b"][0], tabs["a_b"][1])
    sf = sf.reshape(rows, g_n * S5_ST2)
    sb = sb.reshape(rows, g_n * S5_ST2)

    yg = pl.pallas_call(
        _s5_out_body,
        grid=(g_n,),
        in_specs=[
            pl.BlockSpec((1, rows, S5_ROW), lambda i: (i, 0, 0)),
            pl.BlockSpec((1, S5_ROW, S5_ROW), lambda i: (i, 0, 0)),
            pl.BlockSpec((rows, S5_ST2), lambda i: (0, i)),
            pl.BlockSpec((rows, S5_ST2), lambda i: (0, i)),
            pl.BlockSpec((1, S5_ST2, S5_ROW), lambda i: (i, 0, 0)),
            pl.BlockSpec((1, S5_ST2, S5_ROW), lambda i: (i, 0, 0)),
            pl.BlockSpec((1, 1, S5_ROW), lambda i: (i, 0, 0)),
        ],
        out_specs=pl.BlockSpec((1, rows, S5_ROW), lambda i: (i, 0, 0)),
        out_shape=jax.ShapeDtypeStruct((g_n, rows, S5_ROW), mx),
        compiler_params=_cparams(("parallel",)),
        name="s5_out",
    )(ug, tabs["toep"], sf, sb, tabs["wo_f"], tabs["wo_b"], tabs["d_g"])
    y = yg.reshape(g_n, r_n, bsz, ck, S5_GROUP).transpose(2, 1, 3, 0, 4).reshape(t, d)

    tm = _pick(t, 512)
    tn = 512
    n_j = d // tn
    return pl.pallas_call(
        _glu_body,
        grid=(t // tm, n_j),
        in_specs=[
            pl.BlockSpec((tm, d), lambda i, j: (i, 0)),
            pl.BlockSpec((d, tn), lambda i, j: (0, j)),
            pl.BlockSpec((d, tn), lambda i, j: (0, j + n_j)),
            pl.BlockSpec((tm, tn), lambda i, j: (i, j)),
        ],
        out_specs=pl.BlockSpec((tm, tn), lambda i, j: (i, j)),
        out_shape=jax.ShapeDtypeStruct((t, d), F32),
        compiler_params=_cparams(("parallel", "arbitrary")),
        name="s5_glu",
    )(y, w_glu, w_glu, x)


def _log_sigmoid(x):
    return jnp.minimum(x, 0.0) - jnp.log(1.0 + jnp.exp(-jnp.abs(x)))


def _ml_pre_body(xm_ref, prev_ref, next_ref, cw_ref, cb_ref, bdq_ref, bdk_ref, bdkt_ref, bdv_ref, wg_ref, wgt_ref,
                 bg_ref, bgt_ref, q_ref, kt_ref, v_ref, xc_ref, gates_ref, p_ref, s_ref, gatest_ref, pt_ref, st_ref,
                 ext_ref, acc_ref, acct_ref, *, tiles_per_seq):
    i = pl.program_id(0)
    h = pl.program_id(1)
    tm = xm_ref.shape[0]
    pos = lax.rem(i, tiles_per_seq)
    xm_b = xm_ref[...]
    ext_ref[pl.ds(HALO, tm), :] = xm_b.astype(F32)
    ext_ref[pl.ds(0, HALO), :] = jnp.where(pos == 0, 0.0, prev_ref[...].astype(F32))
    ext_ref[pl.ds(HALO + tm, HALO), :] = jnp.where(pos == tiles_per_seq - 1, 0.0, next_ref[...].astype(F32))
    pre = cb_ref[...]
    for jj in range(ML_CONV):
        pre = pre + cw_ref[jj:jj + 1, :] * ext_ref[pl.ds(HALO - ML_CONV // 2 + jj, tm), :]
    xc = (pre * jax.nn.sigmoid(pre)).astype(xm_b.dtype)
    xc_ref[...] = xc
    qh = _mm(xc, bdq_ref[0]).astype(xm_b.dtype)
    kh = _mm(xc, bdk_ref[0]).astype(xm_b.dtype)
    vh = _mm(xm_b, bdv_ref[0]).astype(xm_b.dtype)
    nt = (((1,), (1,)), ((), ()))
    kt = lax.dot_general(bdkt_ref[0], xc, nt, preferred_element_type=F32)
    q_ref[...] = qh
    kt_ref[...] = (kt * (ML_DH ** -0.5)).astype(xm_b.dtype)
    v_ref[...] = vh
    part = _mm(qh, wg_ref[0]) + _mm(kh, wg_ref[1]) + _mm(vh, wg_ref[2])
    part_t = (lax.dot_general(wgt_ref[0], qh, nt, preferred_element_type=F32)
              + lax.dot_general(wgt_ref[1], kh, nt, preferred_element_type=F32)
              + lax.dot_general(wgt_ref[2], vh, nt, preferred_element_type=F32))

    @pl.when(h == 0)
    def _():
        acc_ref[...] = part + bg_ref[...]
        acct_ref[...] = part_t + bgt_ref[...]

    @pl.when(h > 0)
    def _():
        acc_ref[...] += part
        acct_ref[...] += part_t

    @pl.when(h == ML_HEADS - 1)
    def _():
        r_i = lax.broadcasted_iota(jnp.int32, (ML_CHUNK, ML_CHUNK), 0)
        c_i = lax.broadcasted_iota(jnp.int32, (ML_CHUNK, ML_CHUNK), 1)
        lower = (c_i <= r_i).astype(F32)
        upper = (c_i >= r_i).astype(F32)
        gates = acc_ref[...]
        gates_t = acct_ref[...]
        gates_ref[...] = gates
        gatest_ref[...] = gates_t
        for c in range(tm // ML_CHUNK):
            sl = pl.ds(c * ML_CHUNK, ML_CHUNK)
            ls = _log_sigmoid(gates[c * ML_CHUNK:(c + 1) * ML_CHUNK, :])
            p_ref[sl, :] = _mm_exact(lower, ls)
            s_ref[sl, :] = _mm_exact(upper, ls)
            lst = _log_sigmoid(gates_t[:, c * ML_CHUNK:(c + 1) * ML_CHUNK])
            pt_ref[:, sl] = _mm_exact(lst, upper)
            st_ref[:, sl] = _mm_exact(lst, lower)


def _mlstm_body(qf, kf, vf, colf, rowf, qb, kb, vb, colb, rowb, hf_ref, hb_ref, m_ref, *c_refs):
    @pl.when(pl.program_id(2) == 0)
    def _():
        m_ref[...] = jnp.zeros_like(m_ref)
        for c_ref in c_refs:
            c_ref[...] = jnp.zeros_like(c_ref)

    io = ((qf, kf, vf, colf, rowf, hf_ref), (qb, kb, vb, colb, rowb, hb_ref))
    chains = [(hd, d) for hd in range(ML_HP) for d in (0, 1)]
    t_i = lax.broadcasted_iota(jnp.int32, (ML_CHUNK, ML_CHUNK), 0)
    s_i = lax.broadcasted_iota(jnp.int32, (ML_CHUNK, ML_CHUNK), 1)
    masks = (s_i <= t_i, s_i >= t_i)
    m_all = m_ref[...]
    mxd = qf.dtype
    sq = (ML_CHUNK, ML_AUG)
    ones_blk = jnp.ones(sq, mxd)

    st = []
    for i, (hd, d) in enumerate(chains):
        col = io[d][3][hd]
        row = io[d][4][hd]
        bc = col[:, d:d + 1]
        lic = col[:, 2 + d:3 + d]
        bcr = row[d:d + 1, :]
        lir = row[2 + d:3 + d, :]
        g = bcr[:, ML_CHUNK - 1:ML_CHUNK] if d == 0 else bcr[:, 0:1]
        m_prev = m_all[i:i + 1, 0:1]
        bc_b = jnp.broadcast_to(bc, sq)
        a_b = bc_b + m_prev
        drow = lir - bcr
        dm = jnp.where(masks[d], bc_b + drow, -jnp.inf)
        m_t = jnp.maximum(a_b, jnp.max(dm, axis=1, keepdims=True))
        dec = jnp.exp(dm - m_t)
        e = jnp.exp(a_b - m_t)
        m_new = jnp.maximum(g + m_prev, jnp.max(g + drow, axis=1, keepdims=True))
        decay = jnp.exp(g + m_prev - m_new)
        wr = jnp.exp(g - bc_b + jnp.broadcast_to(lic, sq) - m_new)
        st.append(dict(m_t=m_t, dec=dec, e=e, m_new=m_new, decay=decay, wr=wr))

    for i, (hd, d) in enumerate(chains):
        sl = pl.ds(hd * ML_DH, ML_DH)
        s = _mm(io[d][0][:, sl], io[d][1][sl, :]) * st[i]["dec"]
        st[i]["s"] = s.astype(mxd)

    for i, (hd, d) in enumerate(chains):
        sl = pl.ds(hd * ML_DH, ML_DH)
        v_aug = jnp.concatenate([io[d][2][:, sl], ones_blk], axis=1)
        e3 = jnp.concatenate([st[i]["e"]] * (ML_DH // ML_AUG + 1), axis=1)
        nd = e3 * _mm(io[d][0][:, sl], c_refs[i][...].astype(mxd)) + _mm(st[i]["s"], v_aug)
        inv = 1.0 / jnp.maximum(jnp.abs(nd[:, ML_DH:]), jnp.exp(-st[i]["m_t"]))
        inv2 = jnp.concatenate([inv] * (ML_DH // ML_AUG), axis=1)
        io[d][5][:, sl] = (nd[:, :ML_DH] * inv2).astype(io[d][5].dtype)

    for i, (hd, d) in enumerate(chains):
        sl = pl.ds(hd * ML_DH, ML_DH)
        v_aug = jnp.concatenate([io[d][2][:, sl], ones_blk], axis=1)
        wr3 = jnp.concatenate([st[i]["wr"]] * (ML_DH // ML_AUG + 1), axis=1)
        vw = (v_aug.astype(F32) * wr3).astype(mxd)
        c_refs[i][...] = st[i]["decay"] * c_refs[i][...] + _mm(io[d][1][sl, :], vw)
    m_ref[...] = jnp.concatenate([jnp.broadcast_to(c["m_new"], (1, m_all.shape[1])) for c in st], axis=0)


def _ml_out_body(hf_ref, hb_ref, z_ref, xc_ref, ng_ref, sk_ref, w_ref, x_ref, o_ref, act_ref):
    @pl.when(pl.program_id(1) == 0)
    def _():
        for hd in range(ML_HEADS):
            sl = pl.ds(hd * ML_DH, ML_DH)
            hh = hf_ref[:, sl].astype(F32) + hb_ref[:, sl].astype(F32)
            mu = jnp.mean(hh, axis=-1, keepdims=True)
            cen = hh - mu
            var = jnp.mean(cen * cen, axis=-1, keepdims=True)
            hn = cen * lax.rsqrt(var + EPS) * ng_ref[:, sl]
            out = jax.nn.sigmoid(z_ref[:, sl].astype(F32)) * (hn + sk_ref[:, sl] * xc_ref[:, sl].astype(F32))
            act_ref[:, sl] = out.astype(act_ref.dtype)

    o_ref[...] = x_ref[...] + _mm(act_ref[...], w_ref[...])


def _blockdiag(w):
    nb = ML_DH // ML_BLK
    wt = w.astype(F32).reshape(ML_HEADS, nb, ML_BLK, ML_BLK)
    eye = jnp.eye(nb, dtype=F32)
    return jnp.einsum("hncd,nm->hncmd", wt, eye).reshape(ML_HEADS, ML_DH, ML_DH).astype(_MXU_DTYPE)


def _mlstm_mixer(x, bsz, length, g, w_in, conv_w, conv_b, wq, wk, wv, w_gates, b_gates, norm_g, skip, w_out):
    t, d = x.shape
    mx = _MXU_DTYPE
    di, dh, nh = ML_INNER, ML_DH, ML_HEADS
    xz = _rms_mm(x, g, w_in, "ml_in")

    tm = _pick(length, 512)
    tiles_per_seq = length // tm
    hb = tm // HALO
    n_h = t // HALO
    wg = w_gates.astype(mx)
    wgt = wg.transpose(0, 2, 1)
    col_spec = lambda: pl.BlockSpec((tm, dh), lambda i, h: (i, h))
    small = lambda: pl.BlockSpec((tm, ML_NGATE), lambda i, h: (i, 0))
    small_t = lambda: pl.BlockSpec((ML_NGATE, tm), lambda i, h: (0, i))
    bd_spec = lambda: pl.BlockSpec((1, dh, dh), lambda i, h: (h, 0, 0))
    outs = pl.pallas_call(
        functools.partial(_ml_pre_body, tiles_per_seq=tiles_per_seq),
        grid=(t // tm, nh),
        in_specs=[
            col_spec(),
            pl.BlockSpec((HALO, dh), lambda i, h: (jnp.maximum(i * hb - 1, 0), h)),
            pl.BlockSpec((HALO, dh), lambda i, h: (jnp.minimum((i + 1) * hb, n_h - 1), h)),
            pl.BlockSpec((ML_CONV, dh), lambda i, h: (0, h)),
            pl.BlockSpec((1, dh), lambda i, h: (0, h)),
            bd_spec(), bd_spec(), bd_spec(), bd_spec(),
            pl.BlockSpec((3, dh, ML_NGATE), lambda i, h: (0, h, 0)),
            pl.BlockSpec((3, ML_NGATE, dh), lambda i, h: (0, 0, h)),
            pl.BlockSpec((1, ML_NGATE), lambda i, h: (0, 0)),
            pl.BlockSpec((ML_NGATE, 1), lambda i, h: (0, 0)),
        ],
        out_specs=[col_spec(), pl.BlockSpec((dh, tm), lambda i, h: (h, i)), col_spec(), col_spec(),
                   small(), small(), small(), small_t(), small_t(), small_t()],
        out_shape=[jax.ShapeDtypeStruct((t, di), mx), jax.ShapeDtypeStruct((di, t), mx)]
        + [jax.ShapeDtypeStruct((t, di), mx)] * 2 + [jax.ShapeDtypeStruct((t, ML_NGATE), F32)] * 3
        + [jax.ShapeDtypeStruct((ML_NGATE, t), F32)] * 3,
        scratch_shapes=[pltpu.VMEM((tm + 2 * HALO, dh), F32), pltpu.VMEM((tm, ML_NGATE), F32),
                        pltpu.VMEM((ML_NGATE, tm), F32)],
        compiler_params=_cparams(("parallel", "arbitrary")),
        name="ml_pre",
    )(xz, xz, xz, conv_w.astype(F32), conv_b.astype(F32).reshape(1, di), _blockdiag(wq), _blockdiag(wk),
      _blockdiag(wk).transpose(0, 2, 1), _blockdiag(wv), wg, wgt, b_gates.astype(F32).reshape(1, ML_NGATE),
      b_gates.astype(F32).reshape(ML_NGATE, 1))
    q, kt, v, xc, gates, pre, suf, gates_t, pre_t, suf_t = outs
    cols = jnp.stack([pre[:, nh:2 * nh], suf[:, 3 * nh:], gates[:, :nh], gates[:, 2 * nh:3 * nh]], axis=-1)
    cols = cols.transpose(1, 0, 2)
    rows = jnp.stack([pre_t[nh:2 * nh], suf_t[3 * nh:], gates_t[:nh], gates_t[2 * nh:3 * nh]], axis=1)

    nc = length // ML_CHUNK
    hp = ML_HP
    fw = lambda b, h, j: (b * nc + j, h)
    bw = lambda b, h, j: (b * nc + nc - 1 - j, h)
    qkv = lambda f: pl.BlockSpec((ML_CHUNK, hp * dh), f)
    ktspec = lambda f: pl.BlockSpec((hp * dh, ML_CHUNK), lambda b, h, j: f(b, h, j)[::-1])
    colspec = lambda f: pl.BlockSpec((hp, ML_CHUNK, 4), lambda b, h, j: (h, f(b, h, j)[0], 0))
    rowspec = lambda f: pl.BlockSpec((hp, 4, ML_CHUNK), lambda b, h, j: (h, 0, f(b, h, j)[0]))
    hf, hbk = pl.pallas_call(
        _mlstm_body,
        grid=(bsz, nh // hp, nc),
        in_specs=[qkv(fw), ktspec(fw), qkv(fw), colspec(fw), rowspec(fw),
                  qkv(bw), ktspec(bw), qkv(bw), colspec(bw), rowspec(bw)],
        out_specs=[qkv(fw), qkv(bw)],
        out_shape=[jax.ShapeDtypeStruct((t, di), mx)] * 2,
        scratch_shapes=[pltpu.VMEM((2 * hp, 128), F32)] + [pltpu.VMEM((dh, dh + ML_AUG), F32)] * (2 * hp),
        compiler_params=_cparams(("parallel", "parallel", "arbitrary")),
        name="mlstm",
    )(q, kt, v, cols, rows, q, kt, v, cols, rows)

    tmo = _pick(t, 256)
    tn = 512
    return pl.pallas_call(
        _ml_out_body,
        grid=(t // tmo, d // tn),
        in_specs=[
            pl.BlockSpec((tmo, di), lambda i, j: (i, 0)),
            pl.BlockSpec((tmo, di), lambda i, j: (i, 0)),
            pl.BlockSpec((tmo, di), lambda i, j: (i, 1)),
            pl.BlockSpec((tmo, di), lambda i, j: (i, 0)),
            pl.BlockSpec((1, di), lambda i, j: (0, 0)),
            pl.BlockSpec((1, di), lambda i, j: (0, 0)),
            pl.BlockSpec((di, tn), lambda i, j: (0, j)),
            pl.BlockSpec((tmo, tn), lambda i, j: (i, j)),
        ],
        out_specs=pl.BlockSpec((tmo, tn), lambda i, j: (i, j)),
        out_shape=jax.ShapeDtypeStruct((t, d), F32),
        scratch_shapes=[pltpu.VMEM((tmo, di), mx)],
        compiler_params=_cparams(("parallel", "arbitrary")),
        name="ml_out",
    )(hf, hbk, xz, xc, norm_g.astype(F32).reshape(1, di), skip.astype(F32).reshape(1, di), w_out, x)


def _trunk(x3, p):
    bsz, length, d = x3.shape
    x = x3.reshape(bsz * length, d)
    x = _ffn(x, p["norm_g"][0, 0], p["ffn_w_in"][0, 0], p["ffn_w_out"][0, 0])
    x = _s5_mixer(x, bsz, length, p["norm_g"][0, 1], p["s5_w_in"], p["s5_tabs"], p["s5_w_glu"])
    x = _ffn(x, p["norm_g"][0, 2], p["ffn_w_in"][0, 1], p["ffn_w_out"][0, 1])
    x = _ffn(x, p["norm_g"][1, 0], p["ffn_w_in"][1, 0], p["ffn_w_out"][1, 0])
    x = _mlstm_mixer(x, bsz, length, p["norm_g"][1, 1], p["ml_w_in"], *p["ml_rest"])
    x = _ffn(x, p["norm_g"][1, 2], p["ffn_w_in"][1, 1], p["ffn_w_out"][1, 1], final_g=p["final_g"])
    return x.reshape(bsz, length, d)


def kernel(x_prompt, x_sample, norm_g, final_g, ffn_w_in, ffn_w_out, s5_w_in, s5_lambda_re, s5_lambda_im, s5_log_step, s5_b_re, s5_b_im, s5_c_re, s5_c_im, s5_d, s5_w_glu, ml_w_in, ml_conv_w, ml_conv_b, ml_wq, ml_wk, ml_wv, ml_w_gates, ml_b_gates, ml_norm_g, ml_skip, ml_w_out):
    mx = _MXU_DTYPE
    p = dict(
        norm_g=norm_g.astype(F32), final_g=final_g.astype(F32),
        ffn_w_in=ffn_w_in.astype(mx), ffn_w_out=ffn_w_out.astype(mx),
        s5_w_in=s5_w_in[0].astype(mx), s5_w_glu=s5_w_glu[0].astype(mx),
        s5_tabs=_s5_tables(s5_lambda_re[0], s5_lambda_im[0], s5_log_step[0], s5_b_re[0], s5_b_im[0], s5_c_re[0],
                           s5_c_im[0], s5_d[0]),
        ml_w_in=ml_w_in[0].astype(mx),
        ml_rest=(ml_conv_w[0], ml_conv_b[0], ml_wq[0], ml_wk[0], ml_wv[0], ml_w_gates[0], ml_b_gates[0],
                 ml_norm_g[0], ml_skip[0], ml_w_out[0].astype(mx)),
    )
    return (_trunk(x_prompt, p), _trunk(x_sample, p))
```

```python
import functools

import jax
import jax.numpy as jnp
from jax import lax
from jax.experimental import pallas as pl
from jax.experimental.pallas import tpu as pltpu

F32 = jnp.float32
_MXU_DTYPE = jnp.bfloat16

D_MODEL = 2048
D_FF = 5632
EPS = 1e-6
S5_GROUP = 16
S5_GROUPS = D_MODEL // S5_GROUP
S5_STATE = 64
S5_CHUNK = 16
S5_ROW = S5_CHUNK * S5_GROUP
S5_ST2 = 2 * S5_STATE
S5_GL = 8
ML_INNER = 2 * D_MODEL
ML_HEADS = 16
ML_DH = ML_INNER // ML_HEADS
ML_BLK = 4
ML_CONV = 5
ML_CHUNK = 128
ML_NGATE = 4 * ML_HEADS
ML_HP = 4
ML_AUG = 128
HALO = 16

_VMEM_LIMIT = 52 * 1024 * 1024


def _cparams(sem):
    return pltpu.CompilerParams(dimension_semantics=sem, vmem_limit_bytes=_VMEM_LIMIT)


def _mm(a, b):
    return jnp.dot(a, b, preferred_element_type=F32)


def _mm_exact(a, b):
    return jnp.dot(a, b, preferred_element_type=F32, precision=lax.Precision.HIGHEST)


def _rms(x, g):
    ms = jnp.mean(x * x, axis=-1, keepdims=True)
    return x * lax.rsqrt(ms + EPS) * g


def _pick(n, pref):
    t = min(n, pref)
    while n % t:
        t //= 2
    return t


def _ffn_body(x_ref, g_ref, wg_ref, wu_ref, wo_ref, fg_ref, o_ref, hn_ref, *, n_j, final_norm):
    j = pl.program_id(1)

    @pl.when(j == 0)
    def _():
        hn_ref[...] = _rms(x_ref[...], g_ref[...]).astype(hn_ref.dtype)
        o_ref[...] = jnp.zeros_like(o_ref)

    h = hn_ref[...]
    gate = _mm(h, wg_ref[...])
    up = _mm(h, wu_ref[...])
    act = (gate * jax.nn.sigmoid(gate) * up).astype(h.dtype)
    o_ref[...] += _mm(act, wo_ref[...])

    @pl.when(j == n_j - 1)
    def _():
        y = x_ref[...] + 0.5 * o_ref[...]
        if final_norm:
            y = _rms(y, fg_ref[...])
        o_ref[...] = y


def _ffn(x, g, w_in, w_out, final_g=None):
    t, d = x.shape
    tm = _pick(t, 512)
    tn = 512
    n_j = D_FF // tn
    fg = g if final_g is None else final_g
    body = functools.partial(_ffn_body, n_j=n_j, final_norm=final_g is not None)
    return pl.pallas_call(
        body,
        grid=(t // tm, n_j),
        in_specs=[
            pl.BlockSpec((tm, d), lambda i, j: (i, 0)),
            pl.BlockSpec((1, d), lambda i, j: (0, 0)),
            pl.BlockSpec((d, tn), lambda i, j: (0, j)),
            pl.BlockSpec((d, tn), lambda i, j: (0, j + n_j)),
            pl.BlockSpec((tn, d), lambda i, j: (j, 0)),
            pl.BlockSpec((1, d), lambda i, j: (0, 0)),
        ],
        out_specs=pl.BlockSpec((tm, d), lambda i, j: (i, 0)),
        out_shape=jax.ShapeDtypeStruct((t, d), F32),
        scratch_shapes=[pltpu.VMEM((tm, d), _MXU_DTYPE)],
        compiler_params=_cparams(("parallel", "arbitrary")),
        name="ffn",
    )(x, g.reshape(1, d), w_in, w_in, w_out, fg.reshape(1, d))


def _rms_mm_body(x_ref, g_ref, w_ref, o_ref, hn_ref):
    @pl.when(pl.program_id(1) == 0)
    def _():
        hn_ref[...] = _rms(x_ref[...], g_ref[...]).astype(hn_ref.dtype)

    o_ref[...] = _mm(hn_ref[...], w_ref[...]).astype(o_ref.dtype)


def _rms_mm(x, g, w, name, out_dtype=None):
    t, d = x.shape
    n = w.shape[1]
    tm = _pick(t, 1024)
    tn = 1024
    return pl.pallas_call(
        _rms_mm_body,
        grid=(t // tm, n // tn),
        in_specs=[
            pl.BlockSpec((tm, d), lambda i, j: (i, 0)),
            pl.BlockSpec((1, d), lambda i, j: (0, 0)),
            pl.BlockSpec((d, tn), lambda i, j: (0, j)),
        ],
        out_specs=pl.BlockSpec((tm, tn), lambda i, j: (i, j)),
        out_shape=jax.ShapeDtypeStruct((t, n), out_dtype or _MXU_DTYPE),
        scratch_shapes=[pltpu.VMEM((tm, d), _MXU_DTYPE)],
        compiler_params=_cparams(("parallel", "arbitrary")),
        name=name,
    )(x, g.reshape(1, d), w)


def _s5_kern_body(cq_ref, pd_ref, o_ref):
    for n in range(cq_ref.shape[0]):
        o_ref[n] = _mm_exact(cq_ref[n], pd_ref[n])


def _s5_tables(lam_re, lam_im, log_step, b_re, b_im, c_re, c_im, d):
    g_n, p_n, gs, ck = S5_GROUPS, S5_STATE, S5_GROUP, S5_CHUNK
    lam = lax.complex(jnp.minimum(lam_re.astype(F32), -1e-4), lam_im.astype(F32))
    delta = jnp.exp(log_step.astype(F32))[..., None]
    lam_bar = jnp.exp(lam * delta)
    b_bar = ((lam_bar - 1.0) / lam)[..., None] * lax.complex(b_re.astype(F32), b_im.astype(F32))
    c = lax.complex(c_re.astype(F32), c_im.astype(F32))
    pows = [jnp.ones_like(lam_bar)]
    for _ in range(ck):
        pows.append(pows[-1] * lam_bar)
    pw = jnp.stack(pows)
    pw_rev = jnp.stack(pows[::-1])

    x = pw[:ck, :, :, :, None] * b_bar[None]
    xt = x.transpose(1, 2, 3, 0, 4).reshape(2, g_n, p_n, ck * gs)
    pd_t = jnp.concatenate([jnp.real(xt), jnp.imag(xt)], axis=2).reshape(2 * g_n, 2 * p_n, ck * gs)
    cq_t = jnp.concatenate([jnp.real(c), -jnp.imag(c)], axis=-1).reshape(2 * g_n, gs, 2 * p_n)
    gb = 8
    kt = pl.pallas_call(
        _s5_kern_body,
        grid=(2 * g_n // gb,),
        in_specs=[
            pl.BlockSpec((gb, gs, 2 * p_n), lambda i: (i, 0, 0)),
            pl.BlockSpec((gb, 2 * p_n, ck * gs), lambda i: (i, 0, 0)),
        ],
        out_specs=pl.BlockSpec((gb, gs, ck * gs), lambda i: (i, 0, 0)),
        out_shape=jax.ShapeDtypeStruct((2 * g_n, gs, ck * gs), F32),
        compiler_params=_cparams(("parallel",)),
        name="s5_kernels",
    )(cq_t, pd_t)
    kt = kt.reshape(2, g_n, gs, ck, gs)
    s_i = jnp.arange(ck)[:, None]
    t_i = jnp.arange(ck)[None, :]
    dfw = t_i - s_i
    dbw = s_i - t_i
    tf = jnp.where((dfw >= 0)[None, None, :, :, None], kt[0][:, :, jnp.clip(dfw, 0, ck - 1), :], 0.0)
    tb = jnp.where((dbw >= 0)[None, None, :, :, None], kt[1][:, :, jnp.clip(dbw, 0, ck - 1), :], 0.0)
    toep = (tf + tb).transpose(0, 2, 4, 3, 1).reshape(g_n, ck * gs, ck * gs)

    def reim_last(z):
        return jnp.concatenate([jnp.real(z), jnp.imag(z)], axis=-1)

    xf = pw_rev[1:ck + 1, 0, :, :, None] * b_bar[0][None]
    xf = xf.transpose(1, 0, 3, 2).reshape(g_n, ck * gs, p_n)
    xb = x[:, 1].transpose(1, 0, 3, 2).reshape(g_n, ck * gs, p_n)
    w_state = jnp.concatenate([reim_last(xf), reim_last(xb)], axis=-1)

    zf = (c[0][None] * pw[1:ck + 1, 0][:, :, None, :]).transpose(1, 3, 0, 2).reshape(g_n, p_n, ck * gs)
    zb = (c[1][None] * pw_rev[:ck, 1][:, :, None, :]).transpose(1, 3, 0, 2).reshape(g_n, p_n, ck * gs)
    wo_f = jnp.concatenate([jnp.real(zf), -jnp.imag(zf)], axis=1)
    wo_b = jnp.concatenate([jnp.real(zb), -jnp.imag(zb)], axis=1)

    def coef(a):
        return (jnp.concatenate([jnp.real(a), jnp.real(a)], -1), jnp.concatenate([-jnp.imag(a), jnp.imag(a)], -1))

    a_f = coef(pw[ck, 0])
    a_b = coef(pw[ck, 1])
    mx = _MXU_DTYPE
    n8, gl = g_n // S5_GL, S5_GL
    eye = jnp.eye(gl, dtype=mx)

    def diag(w, lead):
        e = eye.reshape((1, gl) + (1,) * (lead + 1) + (gl, 1))
        return w[..., None, :] * e

    toep8 = diag(toep.astype(mx).reshape(n8, gl, ck, gs, ck, gs), 2)
    toep8 = toep8.transpose(0, 2, 1, 3, 4, 5, 6).reshape(n8, ck * gl * gs, ck * gl * gs)
    wst8 = diag(w_state.astype(mx).reshape(n8, gl, ck, gs, 2, 2 * p_n), 2)
    wst8 = wst8.transpose(0, 2, 1, 3, 4, 5, 6).reshape(n8, ck * gl * gs, 2 * gl * 2 * p_n)

    def out8(w):
        w8 = diag(w.astype(mx).reshape(n8, gl, 2 * p_n, ck, gs), 1)
        return w8.reshape(n8, gl * 2 * p_n, ck * gl * gs)

    return dict(toep=toep8, w_state=wst8, wo_f=out8(wo_f), wo_b=out8(wo_b), a_f=a_f, a_b=a_b,
                d=d.astype(F32).reshape(1, g_n * gs))


def _s5_gather_chunks(x_ref, xc_ref):
    rt = xc_ref.shape[0]
    for s in range(S5_CHUNK):
        xc_ref[:, s * 128:(s + 1) * 128] = x_ref[pl.ds(s, rt, stride=S5_CHUNK), :].astype(xc_ref.dtype)


def _s5_state_in_body(u_ref, w_ref, vf_ref, vb_ref, xc_ref):
    _s5_gather_chunks(u_ref, xc_ref)
    v = _mm(xc_ref[...], w_ref[0])
    half = v.shape[1] // 2
    vf_ref[...] = v[:, :half]
    vb_ref[...] = v[:, half:]


def _s5_scan_body(vf_ref, vb_ref, afr_ref, afi_ref, abr_ref, abi_ref, sf_ref, sb_ref, cf_ref, cb_ref, *, rt):
    @pl.when(pl.program_id(1) == 0)
    def _():
        cf_ref[...] = jnp.zeros_like(cf_ref)
        cb_ref[...] = jnp.zeros_like(cb_ref)

    afr, afi, abr, abi = afr_ref[...], afi_ref[...], abr_ref[...], abi_ref[...]

    def step(k, carry):
        s_f, s_b = carry
        sf_ref[0, k] = s_f.astype(sf_ref.dtype)
        s_f = afr * s_f + afi * pltpu.roll(s_f, S5_STATE, 1) + vf_ref[0, k]
        kb = rt - 1 - k
        sb_ref[0, kb] = s_b.astype(sb_ref.dtype)
        s_b = abr * s_b + abi * pltpu.roll(s_b, S5_STATE, 1) + vb_ref[0, kb]
        return s_f, s_b

    s_f, s_b = lax.fori_loop(0, rt, step, (cf_ref[...], cb_ref[...]))
    cf_ref[...] = s_f
    cb_ref[...] = s_b


def _s5_out_body(u_ref, t_ref, sf_ref, sb_ref, wof_ref, wob_ref, d_ref, o_ref, xc_ref):
    _s5_gather_chunks(u_ref, xc_ref)
    rt = xc_ref.shape[0]
    y = _mm(xc_ref[...], t_ref[0]) + _mm(sf_ref[...], wof_ref[0]) + _mm(sb_ref[...], wob_ref[0])
    for s in range(S5_CHUNK):
        rows = pl.ds(s, rt, stride=S5_CHUNK)
        o_ref[rows, :] = jax.nn.gelu(y[:, s * 128:(s + 1) * 128] + d_ref[...] * u_ref[rows, :])


def _glu_body(y_ref, wv_ref, wg_ref, x_ref, o_ref, yb_ref):
    @pl.when(pl.program_id(1) == 0)
    def _():
        yb_ref[...] = y_ref[...].astype(yb_ref.dtype)

    y = yb_ref[...]
    val = _mm(y, wv_ref[...])
    gate = _mm(y, wg_ref[...])
    o_ref[...] = x_ref[...] + val * jax.nn.sigmoid(gate)


def _s5_mixer(x, bsz, length, g, w_in, tabs, w_glu):
    t, d = x.shape
    g_n, ck = S5_GROUPS, S5_CHUNK
    r_n = length // ck
    rows = r_n * bsz
    mx = _MXU_DTYPE
    n8 = g_n // S5_GL
    lanes = S5_GL * S5_GROUP
    k8 = ck * lanes
    st8 = S5_GL * S5_ST2
    u = _rms_mm(x, g, w_in, "s5_in", out_dtype=F32)

    rb = _pick(rows, 256)
    u_spec = pl.BlockSpec((rb * ck, lanes), lambda j, i: (i, j))
    st_spec = lambda: pl.BlockSpec((rb, st8), lambda j, i: (i, j))
    held = lambda shape: pl.BlockSpec((1,) + shape, lambda j, i: (j, 0, 0))
    vf, vb = pl.pallas_call(
        _s5_state_in_body,
        grid=(n8, rows // rb),
        in_specs=[u_spec, held((k8, 2 * st8))],
        out_specs=[st_spec(), st_spec()],
        out_shape=[jax.ShapeDtypeStruct((rows, g_n * S5_ST2), F32)] * 2,
        scratch_shapes=[pltpu.VMEM((rb, k8), mx)],
        compiler_params=_cparams(("parallel", "arbitrary")),
        name="s5_state_in",
    )(u, tabs["w_state"])

    rt = _pick(r_n, 16)
    n_rt = r_n // rt
    v4 = (bsz, r_n, g_n, S5_ST2)
    blk = (1, rt, g_n, S5_ST2)
    fwd = lambda b, i: (b, i, 0, 0)
    bwd = lambda b, i: (b, n_rt - 1 - i, 0, 0)
    coef_spec = pl.BlockSpec((g_n, S5_ST2), lambda b, i: (0, 0))
    sf, sb = pl.pallas_call(
        functools.partial(_s5_scan_body, rt=rt),
        grid=(bsz, n_rt),
        in_specs=[pl.BlockSpec(blk, fwd), pl.BlockSpec(blk, bwd), coef_spec, coef_spec, coef_spec, coef_spec],
        out_specs=[pl.BlockSpec(blk, fwd), pl.BlockSpec(blk, bwd)],
        out_shape=[jax.ShapeDtypeStruct(v4, mx)] * 2,
        scratch_shapes=[pltpu.VMEM((g_n, S5_ST2), F32)] * 2,
        compiler_params=_cparams(("parallel", "arbitrary")),
        name="s5_scan",
    )(vf.reshape(v4), vb.reshape(v4), tabs["a_f"][0], tabs["a_f"][1], tabs["a_b"][0], tabs["a_b"][1])
    sf = sf.reshape(rows, g_n * S5_ST2)
    sb = sb.reshape(rows, g_n * S5_ST2)

    y = pl.pallas_call(
        _s5_out_body,
        grid=(n8, rows // rb),
        in_specs=[u_spec, held((k8, k8)), st_spec(), st_spec(), held((st8, k8)), held((st8, k8)),
                  pl.BlockSpec((1, lanes), lambda j, i: (0, j))],
        out_specs=u_spec,
        out_shape=jax.ShapeDtypeStruct((t, d), F32),
        scratch_shapes=[pltpu.VMEM((rb, k8), mx)],
        compiler_params=_cparams(("parallel", "arbitrary")),
        name="s5_out",
    )(u, tabs["toep"], sf, sb, tabs["wo_f"], tabs["wo_b"], tabs["d"])

    tm = _pick(t, 1024)
    tn = 512
    n_j = d // tn
    return pl.pallas_call(
        _glu_body,
        grid=(t // tm, n_j),
        in_specs=[
            pl.BlockSpec((tm, d), lambda i, j: (i, 0)),
            pl.BlockSpec((d, tn), lambda i, j: (0, j)),
            pl.BlockSpec((d, tn), lambda i, j: (0, j + n_j)),
            pl.BlockSpec((tm, tn), lambda i, j: (i, j)),
        ],
        out_specs=pl.BlockSpec((tm, tn), lambda i, j: (i, j)),
        out_shape=jax.ShapeDtypeStruct((t, d), F32),
        scratch_shapes=[pltpu.VMEM((tm, d), mx)],
        compiler_params=_cparams(("parallel", "arbitrary")),
        name="s5_glu",
    )(y, w_glu, w_glu, x)


def _log_sigmoid(x):
    return jnp.minimum(x, 0.0) - jnp.log(1.0 + jnp.exp(-jnp.abs(x)))


def _ml_pre_body(xm_ref, prev_ref, next_ref, cw_ref, cb_ref, bdq_ref, bdk_ref, bdkt_ref, bdv_ref, wg_ref, wgt_ref,
                 bg_ref, bgt_ref, q_ref, kt_ref, v_ref, xc_ref, gates_ref, p_ref, s_ref, gatest_ref, pt_ref, st_ref,
                 ext_ref, acc_ref, acct_ref, *, tiles_per_seq):
    i = pl.program_id(0)
    h = pl.program_id(1)
    tm = xm_ref.shape[0]
    pos = lax.rem(i, tiles_per_seq)
    xm_b = xm_ref[...]
    ext_ref[pl.ds(HALO, tm), :] = xm_b.astype(F32)
    ext_ref[pl.ds(0, HALO), :] = jnp.where(pos == 0, 0.0, prev_ref[...].astype(F32))
    ext_ref[pl.ds(HALO + tm, HALO), :] = jnp.where(pos == tiles_per_seq - 1, 0.0, next_ref[...].astype(F32))
    pre = cb_ref[...]
    for jj in range(ML_CONV):
        pre = pre + cw_ref[jj:jj + 1, :] * ext_ref[pl.ds(HALO - ML_CONV // 2 + jj, tm), :]
    xc = (pre * jax.nn.sigmoid(pre)).astype(xm_b.dtype)
    xc_ref[...] = xc
    qh = _mm(xc, bdq_ref[0]).astype(xm_b.dtype)
    kh = _mm(xc, bdk_ref[0]).astype(xm_b.dtype)
    vh = _mm(xm_b, bdv_ref[0]).astype(xm_b.dtype)
    nt = (((1,), (1,)), ((), ()))
    kt = lax.dot_general(bdkt_ref[0], xc, nt, preferred_element_type=F32)
    q_ref[...] = qh
    kt = (kt * (ML_DH ** -0.5)).astype(xm_b.dtype)
    for c in range(tm // ML_CHUNK):
        kt_ref[c] = kt[:, c * ML_CHUNK:(c + 1) * ML_CHUNK]
    v_ref[...] = vh
    part = _mm(qh, wg_ref[0]) + _mm(kh, wg_ref[1]) + _mm(vh, wg_ref[2])
    part_t = (lax.dot_general(wgt_ref[0], qh, nt, preferred_element_type=F32)
              + lax.dot_general(wgt_ref[1], kh, nt, preferred_element_type=F32)
              + lax.dot_general(wgt_ref[2], vh, nt, preferred_element_type=F32))

    @pl.when(h == 0)
    def _():
        acc_ref[...] = part + bg_ref[...]
        acct_ref[...] = part_t + bgt_ref[...]

    @pl.when(h > 0)
    def _():
        acc_ref[...] += part
        acct_ref[...] += part_t

    @pl.when(h == ML_HEADS - 1)
    def _():
        r_i = lax.broadcasted_iota(jnp.int32, (ML_CHUNK, ML_CHUNK), 0)
        c_i = lax.broadcasted_iota(jnp.int32, (ML_CHUNK, ML_CHUNK), 1)
        lower = (c_i <= r_i).astype(F32)
        upper = (c_i >= r_i).astype(F32)
        gates = acc_ref[...]
        gates_t = acct_ref[...]
        gates_ref[...] = gates
        gatest_ref[...] = gates_t
        for c in range(tm // ML_CHUNK):
            sl = pl.ds(c * ML_CHUNK, ML_CHUNK)
            ls = _log_sigmoid(gates[c * ML_CHUNK:(c + 1) * ML_CHUNK, :])
            p_ref[sl, :] = _mm_exact(lower, ls)
            s_ref[sl, :] = _mm_exact(upper, ls)
            lst = _log_sigmoid(gates_t[:, c * ML_CHUNK:(c + 1) * ML_CHUNK])
            pt_ref[:, sl] = _mm_exact(lst, upper)
            st_ref[:, sl] = _mm_exact(lst, lower)


def _mlstm_body(qf, kf, vf, colf, rowf, qb, kb, vb, colb, rowb, hf_ref, hb_ref, m_ref, *c_refs):
    @pl.when(pl.program_id(2) == 0)
    def _():
        m_ref[...] = jnp.zeros_like(m_ref)
        for c_ref in c_refs:
            c_ref[...] = jnp.zeros_like(c_ref)

    io = ((qf, kf, vf, colf, rowf, hf_ref), (qb, kb, vb, colb, rowb, hb_ref))
    chains = [(hd, d) for hd in range(ML_HP) for d in (0, 1)]
    t_i = lax.broadcasted_iota(jnp.int32, (ML_CHUNK, ML_CHUNK), 0)
    s_i = lax.broadcasted_iota(jnp.int32, (ML_CHUNK, ML_CHUNK), 1)
    masks = (s_i <= t_i, s_i >= t_i)
    m_all = m_ref[...]
    mxd = qf.dtype
    sq = (ML_CHUNK, ML_AUG)
    ones_blk = jnp.ones(sq, mxd)

    st = []
    for i, (hd, d) in enumerate(chains):
        col = io[d][3][hd]
        row = io[d][4][hd]
        bc = col[:, d:d + 1]
        lic = col[:, 2 + d:3 + d]
        bcr = row[d:d + 1, :]
        lir = row[2 + d:3 + d, :]
        g = bcr[:, ML_CHUNK - 1:ML_CHUNK] if d == 0 else bcr[:, 0:1]
        m_prev = m_all[i:i + 1, 0:1]
        bc_b = jnp.broadcast_to(bc, sq)
        a_b = bc_b + m_prev
        drow = lir - bcr
        dm = jnp.where(masks[d], bc_b + drow, -jnp.inf)
        m_t = jnp.maximum(a_b, jnp.max(dm, axis=1, keepdims=True))
        dec = jnp.exp(dm - m_t)
        e = jnp.exp(a_b - m_t)
        m_new = jnp.maximum(g + m_prev, jnp.max(g + drow, axis=1, keepdims=True))
        decay = jnp.exp(g + m_prev - m_new)
        wr = jnp.exp(g - bc_b + jnp.broadcast_to(lic, sq) - m_new)
        st.append(dict(m_t=m_t, dec=dec, e=e, m_new=m_new, decay=decay, wr=wr))

    for i, (hd, d) in enumerate(chains):
        sl = pl.ds(hd * ML_DH, ML_DH)
        s = _mm(io[d][0][:, sl], io[d][1][sl, :]) * st[i]["dec"]
        st[i]["s"] = s.astype(mxd)

    for i, (hd, d) in enumerate(chains):
        sl = pl.ds(hd * ML_DH, ML_DH)
        v_aug = jnp.concatenate([io[d][2][:, sl], ones_blk], axis=1)
        e3 = jnp.concatenate([st[i]["e"]] * (ML_DH // ML_AUG + 1), axis=1)
        nd = e3 * _mm(io[d][0][:, sl], c_refs[i][...].astype(mxd)) + _mm(st[i]["s"], v_aug)
        inv = 1.0 / jnp.maximum(jnp.abs(nd[:, ML_DH:]), jnp.exp(-st[i]["m_t"]))
        inv2 = jnp.concatenate([inv] * (ML_DH // ML_AUG), axis=1)
        io[d][5][:, sl] = (nd[:, :ML_DH] * inv2).astype(io[d][5].dtype)

    for i, (hd, d) in enumerate(chains):
        sl = pl.ds(hd * ML_DH, ML_DH)
        v_aug = jnp.concatenate([io[d][2][:, sl], ones_blk], axis=1)
        wr3 = jnp.concatenate([st[i]["wr"]] * (ML_DH // ML_AUG + 1), axis=1)
        vw = (v_aug.astype(F32) * wr3).astype(mxd)
        c_refs[i][...] = st[i]["decay"] * c_refs[i][...] + _mm(io[d][1][sl, :], vw)
    m_ref[...] = jnp.concatenate([jnp.broadcast_to(c["m_new"], (1, m_all.shape[1])) for c in st], axis=0)


def _ml_out_body(hf_ref, hb_ref, z_ref, xc_ref, ng_ref, sk_ref, w_ref, x_ref, o_ref, act_ref):
    @pl.when(pl.program_id(1) == 0)
    def _():
        for hd in range(ML_HEADS):
            sl = pl.ds(hd * ML_DH, ML_DH)
            hh = hf_ref[:, sl].astype(F32) + hb_ref[:, sl].astype(F32)
            mu = jnp.mean(hh, axis=-1, keepdims=True)
            cen = hh - mu
            var = jnp.mean(cen * cen, axis=-1, keepdims=True)
            hn = cen * lax.rsqrt(var + EPS) * ng_ref[:, sl]
            out = jax.nn.sigmoid(z_ref[:, sl].astype(F32)) * (hn + sk_ref[:, sl] * xc_ref[:, sl].astype(F32))
            act_ref[:, sl] = out.astype(act_ref.dtype)

    o_ref[...] = x_ref[...] + _mm(act_ref[...], w_ref[...])


def _blockdiag(w):
    nb = ML_DH // ML_BLK
    wt = w.astype(F32).reshape(ML_HEADS, nb, ML_BLK, ML_BLK)
    eye = jnp.eye(nb, dtype=F32)
    return jnp.einsum("hncd,nm->hncmd", wt, eye).reshape(ML_HEADS, ML_DH, ML_DH).astype(_MXU_DTYPE)


def _mlstm_mixer(x, bsz, length, g, w_in, conv_w, conv_b, wq, wk, wv, w_gates, b_gates, norm_g, skip, w_out):
    t, d = x.shape
    mx = _MXU_DTYPE
    di, dh, nh = ML_INNER, ML_DH, ML_HEADS
    xz = _rms_mm(x, g, w_in, "ml_in")

    tm = _pick(length, 512)
    tiles_per_seq = length // tm
    hb = tm // HALO
    n_h = t // HALO
    wg = w_gates.astype(mx)
    wgt = wg.transpose(0, 2, 1)
    col_spec = lambda: pl.BlockSpec((tm, dh), lambda i, h: (i, h))
    small = lambda: pl.BlockSpec((tm, ML_NGATE), lambda i, h: (i, 0))
    small_t = lambda: pl.BlockSpec((ML_NGATE, tm), lambda i, h: (0, i))
    bd_spec = lambda: pl.BlockSpec((1, dh, dh), lambda i, h: (h, 0, 0))
    outs = pl.pallas_call(
        functools.partial(_ml_pre_body, tiles_per_seq=tiles_per_seq),
        grid=(t // tm, nh),
        in_specs=[
            col_spec(),
            pl.BlockSpec((HALO, dh), lambda i, h: (jnp.maximum(i * hb - 1, 0), h)),
            pl.BlockSpec((HALO, dh), lambda i, h: (jnp.minimum((i + 1) * hb, n_h - 1), h)),
            pl.BlockSpec((ML_CONV, dh), lambda i, h: (0, h)),
            pl.BlockSpec((1, dh), lambda i, h: (0, h)),
            bd_spec(), bd_spec(), bd_spec(), bd_spec(),
            pl.BlockSpec((3, dh, ML_NGATE), lambda i, h: (0, h, 0)),
            pl.BlockSpec((3, ML_NGATE, dh), lambda i, h: (0, 0, h)),
            pl.BlockSpec((1, ML_NGATE), lambda i, h: (0, 0)),
            pl.BlockSpec((ML_NGATE, 1), lambda i, h: (0, 0)),
        ],
        out_specs=[col_spec(), pl.BlockSpec((tm // ML_CHUNK, dh, ML_CHUNK), lambda i, h: (i, h, 0)), col_spec(),
                   col_spec(),
                   small(), small(), small(), small_t(), small_t(), small_t()],
        out_shape=[jax.ShapeDtypeStruct((t, di), mx), jax.ShapeDtypeStruct((t // ML_CHUNK, di, ML_CHUNK), mx)]
        + [jax.ShapeDtypeStruct((t, di), mx)] * 2 + [jax.ShapeDtypeStruct((t, ML_NGATE), F32)] * 3
        + [jax.ShapeDtypeStruct((ML_NGATE, t), F32)] * 3,
        scratch_shapes=[pltpu.VMEM((tm + 2 * HALO, dh), F32), pltpu.VMEM((tm, ML_NGATE), F32),
                        pltpu.VMEM((ML_NGATE, tm), F32)],
        compiler_params=_cparams(("parallel", "arbitrary")),
        name="ml_pre",
    )(xz, xz, xz, conv_w.astype(F32), conv_b.astype(F32).reshape(1, di), _blockdiag(wq), _blockdiag(wk),
      _blockdiag(wk).transpose(0, 2, 1), _blockdiag(wv), wg, wgt, b_gates.astype(F32).reshape(1, ML_NGATE),
      b_gates.astype(F32).reshape(ML_NGATE, 1))
    q, kt, v, xc, gates, pre, suf, gates_t, pre_t, suf_t = outs
    cols = jnp.stack([pre[:, nh:2 * nh], suf[:, 3 * nh:], gates[:, :nh], gates[:, 2 * nh:3 * nh]], axis=-1)
    cols = cols.transpose(1, 0, 2)
    rows = jnp.stack([pre_t[nh:2 * nh], suf_t[3 * nh:], gates_t[:nh], gates_t[2 * nh:3 * nh]], axis=1)

    nc = length // ML_CHUNK
    hp = ML_HP
    fw = lambda b, h, j: (b * nc + j, h)
    bw = lambda b, h, j: (b * nc + nc - 1 - j, h)
    qkv = lambda f: pl.BlockSpec((ML_CHUNK, hp * dh), f)
    ktspec = lambda f: pl.BlockSpec((None, hp * dh, ML_CHUNK), lambda b, h, j: f(b, h, j) + (0,))
    colspec = lambda f: pl.BlockSpec((hp, ML_CHUNK, 4), lambda b, h, j: (h, f(b, h, j)[0], 0))
    rowspec = lambda f: pl.BlockSpec((hp, 4, ML_CHUNK), lambda b, h, j: (h, 0, f(b, h, j)[0]))
    hf, hbk = pl.pallas_call(
        _mlstm_body,
        grid=(bsz, nh // hp, nc),
        in_specs=[qkv(fw), ktspec(fw), qkv(fw), colspec(fw), rowspec(fw),
                  qkv(bw), ktspec(bw), qkv(bw), colspec(bw), rowspec(bw)],
        out_specs=[qkv(fw), qkv(bw)],
        out_shape=[jax.ShapeDtypeStruct((t, di), mx)] * 2,
        scratch_shapes=[pltpu.VMEM((2 * hp, 128), F32)] + [pltpu.VMEM((dh, dh + ML_AUG), F32)] * (2 * hp),
        compiler_params=_cparams(("parallel", "parallel", "arbitrary")),
        name="mlstm",
    )(q, kt, v, cols, rows, q, kt, v, cols, rows)

    tmo = _pick(t, 256)
    tn = 512
    return pl.pallas_call(
        _ml_out_body,
        grid=(t // tmo, d // tn),
        in_specs=[
            pl.BlockSpec((tmo, di), lambda i, j: (i, 0)),
            pl.BlockSpec((tmo, di), lambda i, j: (i, 0)),
            pl.BlockSpec((tmo, di), lambda i, j: (i, 1)),
            pl.BlockSpec((tmo, di), lambda i, j: (i, 0)),
            pl.BlockSpec((1, di), lambda i, j: (0, 0)),
            pl.BlockSpec((1, di), lambda i, j: (0, 0)),
            pl.BlockSpec((di, tn), lambda i, j: (0, j)),
            pl.BlockSpec((tmo, tn), lambda i, j: (i, j)),
        ],
        out_specs=pl.BlockSpec((tmo, tn), lambda i, j: (i, j)),
        out_shape=jax.ShapeDtypeStruct((t, d), F32),
        scratch_shapes=[pltpu.VMEM((tmo, di), mx)],
        compiler_params=_cparams(("parallel", "arbitrary")),
        name="ml_out",
    )(hf, hbk, xz, xc, norm_g.astype(F32).reshape(1, di), skip.astype(F32).reshape(1, di), w_out, x)


def _trunk(x3, p):
    bsz, length, d = x3.shape
    x = x3.reshape(bsz * length, d)
    x = _ffn(x, p["norm_g"][0, 0], p["ffn_w_in"][0, 0], p["ffn_w_out"][0, 0])
    x = _s5_mixer(x, bsz, length, p["norm_g"][0, 1], p["s5_w_in"], p["s5_tabs"], p["s5_w_glu"])
    x = _ffn(x, p["norm_g"][0, 2], p["ffn_w_in"][0, 1], p["ffn_w_out"][0, 1])
    x = _ffn(x, p["norm_g"][1, 0], p["ffn_w_in"][1, 0], p["ffn_w_out"][1, 0])
    x = _mlstm_mixer(x, bsz, length, p["norm_g"][1, 1], p["ml_w_in"], *p["ml_rest"])
    x = _ffn(x, p["norm_g"][1, 2], p["ffn_w_in"][1, 1], p["ffn_w_out"][1, 1], final_g=p["final_g"])
    return x.reshape(bsz, length, d)


def kernel(x_prompt, x_sample, norm_g, final_g, ffn_w_in, ffn_w_out, s5_w_in, s5_lambda_re, s5_lambda_im, s5_log_step, s5_b_re, s5_b_im, s5_c_re, s5_c_im, s5_d, s5_w_glu, ml_w_in, ml_conv_w, ml_conv_b, ml_wq, ml_wk, ml_wv, ml_w_gates, ml_b_gates, ml_norm_g, ml_skip, ml_w_out):
    mx = _MXU_DTYPE
    p = dict(
        norm_g=norm_g.astype(F32), final_g=final_g.astype(F32),
        ffn_w_in=ffn_w_in.astype(mx), ffn_w_out=ffn_w_out.astype(mx),
        s5_w_in=s5_w_in[0].astype(mx), s5_w_glu=s5_w_glu[0].astype(mx),
        s5_tabs=_s5_tables(s5_lambda_re[0], s5_lambda_im[0], s5_log_step[0], s5_b_re[0], s5_b_im[0], s5_c_re[0],
                           s5_c_im[0], s5_d[0]),
        ml_w_in=ml_w_in[0].astype(mx),
        ml_rest=(ml_conv_w[0], ml_conv_b[0], ml_wq[0], ml_wk[0], ml_wv[0], ml_w_gates[0], ml_b_gates[0],
                 ml_norm_g[0], ml_skip[0], ml_w_out[0].astype(mx)),
    )
    return (_trunk(x_prompt, p), _trunk(x_sample, p))
```

```python
import functools

import jax
import jax.numpy as jnp
from jax import lax
from jax.experimental import pallas as pl
from jax.experimental.pallas import tpu as pltpu

F32 = jnp.float32
_MXU_DTYPE = jnp.bfloat16

D_MODEL = 2048
D_FF = 5632
EPS = 1e-6
S5_GROUP = 16
S5_GROUPS = D_MODEL // S5_GROUP
S5_STATE = 64
S5_CHUNK = 16
S5_ROW = S5_CHUNK * S5_GROUP
S5_ST2 = 2 * S5_STATE
S5_GL = 8
ML_INNER = 2 * D_MODEL
ML_HEADS = 16
ML_DH = ML_INNER // ML_HEADS
ML_BLK = 4
ML_CONV = 5
ML_CHUNK = 128
ML_NGATE = 4 * ML_HEADS
ML_HP = 8
ML_OG = 4
ML_AUG = 128
HALO = 16

_VMEM_LIMIT = 52 * 1024 * 1024


def _cparams(sem):
    return pltpu.CompilerParams(dimension_semantics=sem, vmem_limit_bytes=_VMEM_LIMIT)


def _mm(a, b):
    return jnp.dot(a, b, preferred_element_type=F32)


def _mm_exact(a, b):
    return jnp.dot(a, b, preferred_element_type=F32, precision=lax.Precision.HIGHEST)


def _rms(x, g):
    ms = jnp.mean(x * x, axis=-1, keepdims=True)
    return x * lax.rsqrt(ms + EPS) * g


def _pick(n, pref):
    t = min(n, pref)
    while n % t:
        t //= 2
    return t


def _ffn_body(x_ref, g_ref, wg_ref, wu_ref, wo_ref, fg_ref, o_ref, hn_ref, *, n_j, final_norm):
    j = pl.program_id(1)

    @pl.when(j == 0)
    def _():
        hn_ref[...] = _rms(x_ref[...], g_ref[...]).astype(hn_ref.dtype)
        o_ref[...] = jnp.zeros_like(o_ref)

    h = hn_ref[...]
    gate = _mm(h, wg_ref[...])
    up = _mm(h, wu_ref[...])
    act = (gate * jax.nn.sigmoid(gate) * up).astype(h.dtype)
    o_ref[...] += _mm(act, wo_ref[...])

    @pl.when(j == n_j - 1)
    def _():
        y = x_ref[...] + 0.5 * o_ref[...]
        if final_norm:
            y = _rms(y, fg_ref[...])
        o_ref[...] = y


def _ffn(x, g, w_in, w_out, final_g=None):
    t, d = x.shape
    tm = _pick(t, 512)
    tn = 512
    n_j = D_FF // tn
    fg = g if final_g is None else final_g
    body = functools.partial(_ffn_body, n_j=n_j, final_norm=final_g is not None)
    return pl.pallas_call(
        body,
        grid=(t // tm, n_j),
        in_specs=[
            pl.BlockSpec((tm, d), lambda i, j: (i, 0)),
            pl.BlockSpec((1, d), lambda i, j: (0, 0)),
            pl.BlockSpec((d, tn), lambda i, j: (0, j)),
            pl.BlockSpec((d, tn), lambda i, j: (0, j + n_j)),
            pl.BlockSpec((tn, d), lambda i, j: (j, 0)),
            pl.BlockSpec((1, d), lambda i, j: (0, 0)),
        ],
        out_specs=pl.BlockSpec((tm, d), lambda i, j: (i, 0)),
        out_shape=jax.ShapeDtypeStruct((t, d), F32),
        scratch_shapes=[pltpu.VMEM((tm, d), _MXU_DTYPE)],
        compiler_params=_cparams(("parallel", "arbitrary")),
        name="ffn",
    )(x, g.reshape(1, d), w_in, w_in, w_out, fg.reshape(1, d))


def _rms_mm_body(x_ref, g_ref, w_ref, o_ref, hn_ref):
    @pl.when(pl.program_id(1) == 0)
    def _():
        hn_ref[...] = _rms(x_ref[...], g_ref[...]).astype(hn_ref.dtype)

    o_ref[...] = _mm(hn_ref[...], w_ref[...]).astype(o_ref.dtype)


def _rms_mm(x, g, w, name, out_dtype=None):
    t, d = x.shape
    n = w.shape[1]
    tm = _pick(t, 1024)
    tn = 1024
    return pl.pallas_call(
        _rms_mm_body,
        grid=(t // tm, n // tn),
        in_specs=[
            pl.BlockSpec((tm, d), lambda i, j: (i, 0)),
            pl.BlockSpec((1, d), lambda i, j: (0, 0)),
            pl.BlockSpec((d, tn), lambda i, j: (0, j)),
        ],
        out_specs=pl.BlockSpec((tm, tn), lambda i, j: (i, j)),
        out_shape=jax.ShapeDtypeStruct((t, n), out_dtype or _MXU_DTYPE),
        scratch_shapes=[pltpu.VMEM((tm, d), _MXU_DTYPE)],
        compiler_params=_cparams(("parallel", "arbitrary")),
        name=name,
    )(x, g.reshape(1, d), w)


def _s5_kern_body(cq_ref, pd_ref, o_ref):
    for n in range(cq_ref.shape[0]):
        o_ref[n] = _mm_exact(cq_ref[n], pd_ref[n])


def _s5_tables(lam_re, lam_im, log_step, b_re, b_im, c_re, c_im, d):
    g_n, p_n, gs, ck = S5_GROUPS, S5_STATE, S5_GROUP, S5_CHUNK
    lam = lax.complex(jnp.minimum(lam_re.astype(F32), -1e-4), lam_im.astype(F32))
    delta = jnp.exp(log_step.astype(F32))[..., None]
    lam_bar = jnp.exp(lam * delta)
    b_bar = ((lam_bar - 1.0) / lam)[..., None] * lax.complex(b_re.astype(F32), b_im.astype(F32))
    c = lax.complex(c_re.astype(F32), c_im.astype(F32))
    pows = [jnp.ones_like(lam_bar)]
    for _ in range(ck):
        pows.append(pows[-1] * lam_bar)
    pw = jnp.stack(pows)
    pw_rev = jnp.stack(pows[::-1])

    x = pw[:ck, :, :, :, None] * b_bar[None]
    xt = x.transpose(1, 2, 3, 0, 4).reshape(2, g_n, p_n, ck * gs)
    pd_t = jnp.concatenate([jnp.real(xt), jnp.imag(xt)], axis=2).reshape(2 * g_n, 2 * p_n, ck * gs)
    cq_t = jnp.concatenate([jnp.real(c), -jnp.imag(c)], axis=-1).reshape(2 * g_n, gs, 2 * p_n)
    gb = 8
    kt = pl.pallas_call(
        _s5_kern_body,
        grid=(2 * g_n // gb,),
        in_specs=[
            pl.BlockSpec((gb, gs, 2 * p_n), lambda i: (i, 0, 0)),
            pl.BlockSpec((gb, 2 * p_n, ck * gs), lambda i: (i, 0, 0)),
        ],
        out_specs=pl.BlockSpec((gb, gs, ck * gs), lambda i: (i, 0, 0)),
        out_shape=jax.ShapeDtypeStruct((2 * g_n, gs, ck * gs), F32),
        compiler_params=_cparams(("parallel",)),
        name="s5_kernels",
    )(cq_t, pd_t)
    kt = kt.reshape(2, g_n, gs, ck, gs)
    s_i = jnp.arange(ck)[:, None]
    t_i = jnp.arange(ck)[None, :]
    dfw = t_i - s_i
    dbw = s_i - t_i
    tf = jnp.where((dfw >= 0)[None, None, :, :, None], kt[0][:, :, jnp.clip(dfw, 0, ck - 1), :], 0.0)
    tb = jnp.where((dbw >= 0)[None, None, :, :, None], kt[1][:, :, jnp.clip(dbw, 0, ck - 1), :], 0.0)
    toep = (tf + tb).transpose(0, 2, 4, 3, 1).reshape(g_n, ck * gs, ck * gs)

    def reim_last(z):
        return jnp.concatenate([jnp.real(z), jnp.imag(z)], axis=-1)

    xf = pw_rev[1:ck + 1, 0, :, :, None] * b_bar[0][None]
    xf = xf.transpose(1, 0, 3, 2).reshape(g_n, ck * gs, p_n)
    xb = x[:, 1].transpose(1, 0, 3, 2).reshape(g_n, ck * gs, p_n)
    w_state = jnp.concatenate([reim_last(xf), reim_last(xb)], axis=-1)

    zf = (c[0][None] * pw[1:ck + 1, 0][:, :, None, :]).transpose(1, 3, 0, 2).reshape(g_n, p_n, ck * gs)
    zb = (c[1][None] * pw_rev[:ck, 1][:, :, None, :]).transpose(1, 3, 0, 2).reshape(g_n, p_n, ck * gs)
    wo_f = jnp.concatenate([jnp.real(zf), -jnp.imag(zf)], axis=1)
    wo_b = jnp.concatenate([jnp.real(zb), -jnp.imag(zb)], axis=1)

    def coef(a):
        return (jnp.concatenate([jnp.real(a), jnp.real(a)], -1), jnp.concatenate([-jnp.imag(a), jnp.imag(a)], -1))

    a_f = coef(pw[ck, 0])
    a_b = coef(pw[ck, 1])
    mx = _MXU_DTYPE
    n8, gl = g_n // S5_GL, S5_GL
    kk = ck * gl * gs

    def rows_sgc(w):
        n = w.shape[-1]
        return w.astype(mx).reshape(n8, gl, ck, gs, n).transpose(0, 2, 1, 3, 4).reshape(n8, kk, n)

    def rows_gp(w):
        return w.astype(mx).reshape(n8, gl * w.shape[1], w.shape[2])

    td = (ck, gs)
    ep = (2, 2 * p_n)
    return dict(toep=_s5_widen(rows_sgc(toep), gs, td), w_state=_s5_widen(rows_sgc(w_state), gs, ep),
                wo_f=_s5_widen(rows_gp(wo_f), 2 * p_n, td), wo_b=_s5_widen(rows_gp(wo_b), 2 * p_n, td),
                a_f=a_f, a_b=a_b, d=d.astype(F32).reshape(1, g_n * gs))


def _s5_widen_body(w_ref, e_ref, o_ref, *, row_div, col_div):
    y = _mm(w_ref[0], e_ref[...])
    r_g = (lax.broadcasted_iota(jnp.int32, y.shape, 0) // row_div) % S5_GL
    c_g = ((lax.broadcasted_iota(jnp.int32, y.shape, 1) + pl.program_id(1) * y.shape[1]) // col_div) % S5_GL
    o_ref[0] = jnp.where(r_g == c_g, y, 0.0).astype(o_ref.dtype)


def _s5_widen(w, row_div, col_dims):
    n8, rows, n = w.shape
    a, b = col_dims
    e = jnp.broadcast_to(jnp.eye(a * b, dtype=w.dtype).reshape(a, b, a, 1, b), (a, b, a, S5_GL, b))
    e = e.reshape(n, n * S5_GL)
    tn = 512
    return pl.pallas_call(
        functools.partial(_s5_widen_body, row_div=row_div, col_div=b),
        grid=(n8, n * S5_GL // tn),
        in_specs=[pl.BlockSpec((1, rows, n), lambda x, j: (x, 0, 0)), pl.BlockSpec((n, tn), lambda x, j: (0, j))],
        out_specs=pl.BlockSpec((1, rows, tn), lambda x, j: (x, 0, j)),
        out_shape=jax.ShapeDtypeStruct((n8, rows, n * S5_GL), w.dtype),
        compiler_params=_cparams(("parallel", "arbitrary")),
        name="s5_widen",
    )(w, e)


def _s5_gather_chunks(x_ref, xc_ref):
    rt = xc_ref.shape[0]
    for s in range(S5_CHUNK):
        xc_ref[:, s * 128:(s + 1) * 128] = x_ref[pl.ds(s, rt, stride=S5_CHUNK), :].astype(xc_ref.dtype)


def _s5_state_in_body(u_ref, w_ref, vf_ref, vb_ref, xc_ref):
    _s5_gather_chunks(u_ref, xc_ref)
    v = _mm(xc_ref[...], w_ref[0])
    for n, o_ref in enumerate((vf_ref, vb_ref)):
        for gi in range(S5_GL):
            c0 = (n * S5_GL + gi) * S5_ST2
            o_ref[:, gi, :] = v[:, c0:c0 + S5_ST2]


def _s5_scan_body(vf_ref, vb_ref, afr_ref, afi_ref, abr_ref, abi_ref, sf_ref, sb_ref, cf_ref, cb_ref, *, rt):
    @pl.when(pl.program_id(1) == 0)
    def _():
        cf_ref[...] = jnp.zeros_like(cf_ref)
        cb_ref[...] = jnp.zeros_like(cb_ref)

    afr, afi, abr, abi = afr_ref[...], afi_ref[...], abr_ref[...], abi_ref[...]

    def step(k, carry):
        s_f, s_b = carry
        sf_ref[0, k] = s_f.astype(sf_ref.dtype)
        s_f = afr * s_f + afi * pltpu.roll(s_f, S5_STATE, 1) + vf_ref[0, k]
        kb = rt - 1 - k
        sb_ref[0, kb] = s_b.astype(sb_ref.dtype)
        s_b = abr * s_b + abi * pltpu.roll(s_b, S5_STATE, 1) + vb_ref[0, kb]
        return s_f, s_b

    s_f, s_b = lax.fori_loop(0, rt, step, (cf_ref[...], cb_ref[...]))
    cf_ref[...] = s_f
    cb_ref[...] = s_b


def _s5_out_body(u_ref, t_ref, sf_ref, sb_ref, wof_ref, wob_ref, d_ref, o_ref, xc_ref):
    _s5_gather_chunks(u_ref, xc_ref)
    rt = xc_ref.shape[0]
    mxd = xc_ref.dtype
    s_f = jnp.concatenate([sf_ref[:, gi, :].astype(mxd) for gi in range(S5_GL)], axis=1)
    s_b = jnp.concatenate([sb_ref[:, gi, :].astype(mxd) for gi in range(S5_GL)], axis=1)
    y = _mm(xc_ref[...], t_ref[0]) + _mm(s_f, wof_ref[0]) + _mm(s_b, wob_ref[0])
    for s in range(S5_CHUNK):
        rows = pl.ds(s, rt, stride=S5_CHUNK)
        o_ref[rows, :] = jax.nn.gelu(y[:, s * 128:(s + 1) * 128] + d_ref[...] * u_ref[rows, :])


def _glu_body(y_ref, wv_ref, wg_ref, x_ref, o_ref, yb_ref):
    @pl.when(pl.program_id(1) == 0)
    def _():
        yb_ref[...] = y_ref[...].astype(yb_ref.dtype)

    y = yb_ref[...]
    val = _mm(y, wv_ref[...])
    gate = _mm(y, wg_ref[...])
    o_ref[...] = x_ref[...] + val * jax.nn.sigmoid(gate)


def _s5_mixer(x, bsz, length, g, w_in, tabs, w_glu):
    t, d = x.shape
    g_n, ck = S5_GROUPS, S5_CHUNK
    r_n = length // ck
    rows = r_n * bsz
    mx = _MXU_DTYPE
    n8 = g_n // S5_GL
    lanes = S5_GL * S5_GROUP
    k8 = ck * lanes
    st8 = S5_GL * S5_ST2
    u = _rms_mm(x, g, w_in, "s5_in", out_dtype=F32)

    rb = _pick(rows, 256)
    u_spec = pl.BlockSpec((rb * ck, lanes), lambda j, i: (i, j))
    st_spec = lambda: pl.BlockSpec((rb, S5_GL, S5_ST2), lambda j, i: (i, j, 0))
    held = lambda shape: pl.BlockSpec((1,) + shape, lambda j, i: (j, 0, 0))
    vf, vb = pl.pallas_call(
        _s5_state_in_body,
        grid=(n8, rows // rb),
        in_specs=[u_spec, held((k8, 2 * st8))],
        out_specs=[st_spec(), st_spec()],
        out_shape=[jax.ShapeDtypeStruct((rows, g_n, S5_ST2), F32)] * 2,
        scratch_shapes=[pltpu.VMEM((rb, k8), mx)],
        compiler_params=_cparams(("parallel", "arbitrary")),
        name="s5_state_in",
    )(u, tabs["w_state"])

    rt = _pick(r_n, 16)
    n_rt = r_n // rt
    v4 = (bsz, r_n, g_n, S5_ST2)
    blk = (1, rt, g_n, S5_ST2)
    fwd = lambda b, i: (b, i, 0, 0)
    bwd = lambda b, i: (b, n_rt - 1 - i, 0, 0)
    coef_spec = pl.BlockSpec((g_n, S5_ST2), lambda b, i: (0, 0))
    sf, sb = pl.pallas_call(
        functools.partial(_s5_scan_body, rt=rt),
        grid=(bsz, n_rt),
        in_specs=[pl.BlockSpec(blk, fwd), pl.BlockSpec(blk, bwd), coef_spec, coef_spec, coef_spec, coef_spec],
        out_specs=[pl.BlockSpec(blk, fwd), pl.BlockSpec(blk, bwd)],
        out_shape=[jax.ShapeDtypeStruct(v4, F32)] * 2,
        scratch_shapes=[pltpu.VMEM((g_n, S5_ST2), F32)] * 2,
        compiler_params=_cparams(("parallel", "arbitrary")),
        name="s5_scan",
    )(vf.reshape(v4), vb.reshape(v4), tabs["a_f"][0], tabs["a_f"][1], tabs["a_b"][0], tabs["a_b"][1])
    sf = sf.reshape(rows, g_n, S5_ST2)
    sb = sb.reshape(rows, g_n, S5_ST2)

    y = pl.pallas_call(
        _s5_out_body,
        grid=(n8, rows // rb),
        in_specs=[u_spec, held((k8, k8)), st_spec(), st_spec(), held((st8, k8)), held((st8, k8)),
                  pl.BlockSpec((1, lanes), lambda j, i: (0, j))],
        out_specs=u_spec,
        out_shape=jax.ShapeDtypeStruct((t, d), F32),
        scratch_shapes=[pltpu.VMEM((rb, k8), mx)],
        compiler_params=_cparams(("parallel", "arbitrary")),
        name="s5_out",
    )(u, tabs["toep"], sf, sb, tabs["wo_f"], tabs["wo_b"], tabs["d"])

    tm = _pick(t, 1024)
    tn = 512
    n_j = d // tn
    return pl.pallas_call(
        _glu_body,
        grid=(t // tm, n_j),
        in_specs=[
            pl.BlockSpec((tm, d), lambda i, j: (i, 0)),
            pl.BlockSpec((d, tn), lambda i, j: (0, j)),
            pl.BlockSpec((d, tn), lambda i, j: (0, j + n_j)),
            pl.BlockSpec((tm, tn), lambda i, j: (i, j)),
        ],
        out_specs=pl.BlockSpec((tm, tn), lambda i, j: (i, j)),
        out_shape=jax.ShapeDtypeStruct((t, d), F32),
        scratch_shapes=[pltpu.VMEM((tm, d), mx)],
        compiler_params=_cparams(("parallel", "arbitrary")),
        name="s5_glu",
    )(y, w_glu, w_glu, x)


def _log_sigmoid(x):
    return jnp.minimum(x, 0.0) - jnp.log(1.0 + jnp.exp(-jnp.abs(x)))


def _ml_pre_body(xm_ref, prev_ref, next_ref, cw_ref, cb_ref, bdq_ref, bdkt_ref, bdv_ref, wgt_ref, bgt_ref,
                 q_ref, kt_ref, v_ref, xc_ref, gatest_ref, pt_ref, st_ref, ext_ref, acct_ref, *, tiles_per_seq):
    i = pl.program_id(0)
    h = pl.program_id(1)
    tm = xm_ref.shape[0]
    pos = lax.rem(i, tiles_per_seq)
    xm_b = xm_ref[...]
    ext_ref[pl.ds(HALO, tm), :] = xm_b.astype(F32)
    ext_ref[pl.ds(0, HALO), :] = jnp.where(pos == 0, 0.0, prev_ref[...].astype(F32))
    ext_ref[pl.ds(HALO + tm, HALO), :] = jnp.where(pos == tiles_per_seq - 1, 0.0, next_ref[...].astype(F32))
    pre = cb_ref[...]
    for jj in range(ML_CONV):
        pre = pre + cw_ref[jj:jj + 1, :] * ext_ref[pl.ds(HALO - ML_CONV // 2 + jj, tm), :]
    xc = (pre * jax.nn.sigmoid(pre)).astype(xm_b.dtype)
    xc_ref[...] = xc
    qh = _mm(xc, bdq_ref[0]).astype(xm_b.dtype)
    vh = _mm(xm_b, bdv_ref[0]).astype(xm_b.dtype)
    nt = (((1,), (1,)), ((), ()))
    scale = ML_DH ** -0.5
    kt = (lax.dot_general(bdkt_ref[0], xc, nt, preferred_element_type=F32) * scale).astype(xm_b.dtype)
    q_ref[...] = qh
    for c in range(tm // ML_CHUNK):
        kt_ref[c] = kt[:, c * ML_CHUNK:(c + 1) * ML_CHUNK]
    v_ref[...] = vh
    part_t = (lax.dot_general(wgt_ref[0], qh, nt, preferred_element_type=F32)
              + _mm(wgt_ref[1], kt) * (1.0 / scale)
              + lax.dot_general(wgt_ref[2], vh, nt, preferred_element_type=F32))

    @pl.when(h == 0)
    def _():
        acct_ref[...] = part_t + bgt_ref[...]

    @pl.when(h > 0)
    def _():
        acct_ref[...] += part_t

    @pl.when(h == ML_HEADS - 1)
    def _():
        r_i = lax.broadcasted_iota(jnp.int32, (ML_CHUNK, ML_CHUNK), 0)
        c_i = lax.broadcasted_iota(jnp.int32, (ML_CHUNK, ML_CHUNK), 1)
        incl_before = (r_i <= c_i).astype(F32)
        incl_after = (r_i >= c_i).astype(F32)
        gates_t = acct_ref[...]
        gatest_ref[...] = gates_t
        for c in range(tm // ML_CHUNK):
            sl = pl.ds(c * ML_CHUNK, ML_CHUNK)
            lst = _log_sigmoid(gates_t[:, c * ML_CHUNK:(c + 1) * ML_CHUNK])
            pt_ref[:, sl] = _mm_exact(lst, incl_before)
            st_ref[:, sl] = _mm_exact(lst, incl_after)


def _mlstm_body(qf, kf, vf, rowf, qb, kb, vb, rowb, hf_ref, hb_ref, m_ref, *c_refs):
    @pl.when(pl.program_id(2) == 0)
    def _():
        m_ref[...] = jnp.zeros_like(m_ref)
        for c_ref in c_refs:
            c_ref[...] = jnp.zeros_like(c_ref)

    io = ((qf, kf, vf, rowf, hf_ref), (qb, kb, vb, rowb, hb_ref))
    chains = [(hd, d) for hd in range(ML_HP) for d in (0, 1)]
    t_i = lax.broadcasted_iota(jnp.int32, (ML_CHUNK, ML_CHUNK), 0)
    s_i = lax.broadcasted_iota(jnp.int32, (ML_CHUNK, ML_CHUNK), 1)
    masks = (s_i <= t_i, s_i >= t_i)
    m_all = m_ref[...]
    mxd = qf.dtype
    sq = (ML_CHUNK, ML_AUG)
    ones_blk = jnp.ones(sq, mxd)

    st = []
    for i, (hd, d) in enumerate(chains):
        row = io[d][3][hd]
        col = jnp.concatenate([row, row], axis=0).T
        bc = col[:, d:d + 1]
        lic = col[:, 2 + d:3 + d]
        bcr = row[d:d + 1, :]
        lir = row[2 + d:3 + d, :]
        g = bcr[:, ML_CHUNK - 1:ML_CHUNK] if d == 0 else bcr[:, 0:1]
        m_prev = m_all[i:i + 1, 0:1]
        bc_b = jnp.broadcast_to(bc, sq)
        a_b = bc_b + m_prev
        drow = lir - bcr
        dm = jnp.where(masks[d], bc_b + drow, -jnp.inf)
        m_t = jnp.maximum(a_b, jnp.max(dm, axis=1, keepdims=True))
        dec = jnp.exp(dm - m_t)
        e = jnp.exp(a_b - m_t)
        m_new = jnp.maximum(g + m_prev, jnp.max(g + drow, axis=1, keepdims=True))
        decay = jnp.exp(g + m_prev - m_new)
        wr = jnp.exp(g - bc_b + jnp.broadcast_to(lic, sq) - m_new)
        st.append(dict(m_t=m_t, dec=dec, e=e, m_new=m_new, decay=decay, wr=wr))

    for i, (hd, d) in enumerate(chains):
        sl = pl.ds(hd * ML_DH, ML_DH)
        s = _mm(io[d][0][:, sl], io[d][1][sl, :]) * st[i]["dec"]
        st[i]["s"] = s.astype(mxd)

    for i, (hd, d) in enumerate(chains):
        sl = pl.ds(hd * ML_DH, ML_DH)
        v_aug = jnp.concatenate([io[d][2][:, sl], ones_blk], axis=1)
        e3 = jnp.concatenate([st[i]["e"]] * (ML_DH // ML_AUG + 1), axis=1)
        nd = e3 * _mm(io[d][0][:, sl], c_refs[i][...].astype(mxd)) + _mm(st[i]["s"], v_aug)
        inv = 1.0 / jnp.maximum(jnp.abs(nd[:, ML_DH:]), jnp.exp(-st[i]["m_t"]))
        inv2 = jnp.concatenate([inv] * (ML_DH // ML_AUG), axis=1)
        io[d][4][:, sl] = (nd[:, :ML_DH] * inv2).astype(io[d][4].dtype)

    for i, (hd, d) in enumerate(chains):
        sl = pl.ds(hd * ML_DH, ML_DH)
        v_aug = jnp.concatenate([io[d][2][:, sl], ones_blk], axis=1)
        wr3 = jnp.concatenate([st[i]["wr"]] * (ML_DH // ML_AUG + 1), axis=1)
        vw = (v_aug.astype(F32) * wr3).astype(mxd)
        c_refs[i][...] = st[i]["decay"] * c_refs[i][...] + _mm(io[d][1][sl, :], vw)
    m_ref[...] = jnp.concatenate([jnp.broadcast_to(c["m_new"], (1, m_all.shape[1])) for c in st], axis=0)


def _ml_out_body(hf_ref, hb_ref, z_ref, xc_ref, ng_ref, sk_ref, w_ref, x_ref, o_ref):
    @pl.when(pl.program_id(1) == 0)
    def _():
        o_ref[...] = x_ref[...]

    acc = o_ref[...]
    for hd in range(ML_OG):
        sl = pl.ds(hd * ML_DH, ML_DH)
        hh = hf_ref[:, sl].astype(F32) + hb_ref[:, sl].astype(F32)
        mu = jnp.mean(hh, axis=-1, keepdims=True)
        cen = hh - mu
        var = jnp.mean(cen * cen, axis=-1, keepdims=True)
        hn = cen * lax.rsqrt(var + EPS) * ng_ref[:, sl]
        out = jax.nn.sigmoid(z_ref[:, sl].astype(F32)) * (hn + sk_ref[:, sl] * xc_ref[:, sl].astype(F32))
        acc = acc + _mm(out.astype(w_ref.dtype), w_ref[sl, :])
    o_ref[...] = acc


def _blockdiag(w):
    nb = ML_DH // ML_BLK
    wt = w.astype(F32).reshape(ML_HEADS, nb, ML_BLK, ML_BLK)
    eye = jnp.eye(nb, dtype=F32)
    return jnp.einsum("hncd,nm->hncmd", wt, eye).reshape(ML_HEADS, ML_DH, ML_DH).astype(_MXU_DTYPE)


def _mlstm_mixer(x, bsz, length, g, w_in, conv_w, conv_b, wq, wk, wv, w_gates, b_gates, norm_g, skip, w_out):
    t, d = x.shape
    mx = _MXU_DTYPE
    di, dh, nh = ML_INNER, ML_DH, ML_HEADS
    xz = _rms_mm(x, g, w_in, "ml_in")

    tm = _pick(length, 512)
    tiles_per_seq = length // tm
    hb = tm // HALO
    n_h = t // HALO
    wgt = w_gates.astype(mx).transpose(0, 2, 1)
    col_spec = lambda: pl.BlockSpec((tm, dh), lambda i, h: (i, h))
    small_t = lambda: pl.BlockSpec((ML_NGATE, tm), lambda i, h: (0, i))
    bd_spec = lambda: pl.BlockSpec((1, dh, dh), lambda i, h: (h, 0, 0))
    outs = pl.pallas_call(
        functools.partial(_ml_pre_body, tiles_per_seq=tiles_per_seq),
        grid=(t // tm, nh),
        in_specs=[
            col_spec(),
            pl.BlockSpec((HALO, dh), lambda i, h: (jnp.maximum(i * hb - 1, 0), h)),
            pl.BlockSpec((HALO, dh), lambda i, h: (jnp.minimum((i + 1) * hb, n_h - 1), h)),
            pl.BlockSpec((ML_CONV, dh), lambda i, h: (0, h)),
            pl.BlockSpec((1, dh), lambda i, h: (0, h)),
            bd_spec(), bd_spec(), bd_spec(),
            pl.BlockSpec((3, ML_NGATE, dh), lambda i, h: (0, 0, h)),
            pl.BlockSpec((ML_NGATE, 1), lambda i, h: (0, 0)),
        ],
        out_specs=[col_spec(), pl.BlockSpec((tm // ML_CHUNK, dh, ML_CHUNK), lambda i, h: (i, h, 0)), col_spec(),
                   col_spec(), small_t(), small_t(), small_t()],
        out_shape=[jax.ShapeDtypeStruct((t, di), mx), jax.ShapeDtypeStruct((t // ML_CHUNK, di, ML_CHUNK), mx)]
        + [jax.ShapeDtypeStruct((t, di), mx)] * 2 + [jax.ShapeDtypeStruct((ML_NGATE, t), F32)] * 3,
        scratch_shapes=[pltpu.VMEM((tm + 2 * HALO, dh), F32), pltpu.VMEM((ML_NGATE, tm), F32)],
        compiler_params=_cparams(("parallel", "arbitrary")),
        name="ml_pre",
    )(xz, xz, xz, conv_w.astype(F32), conv_b.astype(F32).reshape(1, di), _blockdiag(wq),
      _blockdiag(wk).transpose(0, 2, 1), _blockdiag(wv), wgt, b_gates.astype(F32).reshape(ML_NGATE, 1))
    q, kt, v, xc, gates_t, pre_t, suf_t = outs
    rows = jnp.stack([pre_t[nh:2 * nh], suf_t[3 * nh:], gates_t[:nh], gates_t[2 * nh:3 * nh]], axis=1)

    nc = length // ML_CHUNK
    hp = ML_HP
    fw = lambda b, h, j: (b * nc + j, h)
    bw = lambda b, h, j: (b * nc + nc - 1 - j, h)
    qkv = lambda f: pl.BlockSpec((ML_CHUNK, hp * dh), f)
    ktspec = lambda f: pl.BlockSpec((None, hp * dh, ML_CHUNK), lambda b, h, j: f(b, h, j) + (0,))
    rowspec = lambda f: pl.BlockSpec((hp, 4, ML_CHUNK), lambda b, h, j: (h, 0, f(b, h, j)[0]))
    hf, hbk = pl.pallas_call(
        _mlstm_body,
        grid=(bsz, nh // hp, nc),
        in_specs=[qkv(fw), ktspec(fw), qkv(fw), rowspec(fw), qkv(bw), ktspec(bw), qkv(bw), rowspec(bw)],
        out_specs=[qkv(fw), qkv(bw)],
        out_shape=[jax.ShapeDtypeStruct((t, di), mx)] * 2,
        scratch_shapes=[pltpu.VMEM((2 * hp, 128), F32)] + [pltpu.VMEM((dh, dh + ML_AUG), F32)] * (2 * hp),
        compiler_params=_cparams(("parallel", "parallel", "arbitrary")),
        name="mlstm",
    )(q, kt, v, rows, q, kt, v, rows)

    tmo = _pick(t, 512)
    kg = ML_OG * dh
    n_g = nh // ML_OG
    grp = lambda: pl.BlockSpec((tmo, kg), lambda i, j: (i, j))
    return pl.pallas_call(
        _ml_out_body,
        grid=(t // tmo, n_g),
        in_specs=[
            grp(), grp(),
            pl.BlockSpec((tmo, kg), lambda i, j: (i, j + n_g)),
            grp(),
            pl.BlockSpec((1, kg), lambda i, j: (0, j)),
            pl.BlockSpec((1, kg), lambda i, j: (0, j)),
            pl.BlockSpec((kg, d), lambda i, j: (j, 0)),
            pl.BlockSpec((tmo, d), lambda i, j: (i, 0)),
        ],
        out_specs=pl.BlockSpec((tmo, d), lambda i, j: (i, 0)),
        out_shape=jax.ShapeDtypeStruct((t, d), F32),
        compiler_params=_cparams(("parallel", "arbitrary")),
        name="ml_out",
    )(hf, hbk, xz, xc, norm_g.astype(F32).reshape(1, di), skip.astype(F32).reshape(1, di), w_out, x)


def _trunk(x3, p):
    bsz, length, d = x3.shape
    x = x3.reshape(bsz * length, d)
    x = _ffn(x, p["norm_g"][0, 0], p["ffn_w_in"][0, 0], p["ffn_w_out"][0, 0])
    x = _s5_mixer(x, bsz, length, p["norm_g"][0, 1], p["s5_w_in"], p["s5_tabs"], p["s5_w_glu"])
    x = _ffn(x, p["norm_g"][0, 2], p["ffn_w_in"][0, 1], p["ffn_w_out"][0, 1])
    x = _ffn(x, p["norm_g"][1, 0], p["ffn_w_in"][1, 0], p["ffn_w_out"][1, 0])
    x = _mlstm_mixer(x, bsz, length, p["norm_g"][1, 1], p["ml_w_in"], *p["ml_rest"])
    x = _ffn(x, p["norm_g"][1, 2], p["ffn_w_in"][1, 1], p["ffn_w_out"][1, 1], final_g=p["final_g"])
    return x.reshape(bsz, length, d)


def kernel(x_prompt, x_sample, norm_g, final_g, ffn_w_in, ffn_w_out, s5_w_in, s5_lambda_re, s5_lambda_im, s5_log_step, s5_b_re, s5_b_im, s5_c_re, s5_c_im, s5_d, s5_w_glu, ml_w_in, ml_conv_w, ml_conv_b, ml_wq, ml_wk, ml_wv, ml_w_gates, ml_b_gates, ml_norm_g, ml_skip, ml_w_out):
    mx = _MXU_DTYPE
    p = dict(
        norm_g=norm_g.astype(F32), final_g=final_g.astype(F32),
        ffn_w_in=ffn_w_in.astype(mx), ffn_w_out=ffn_w_out.astype(mx),
        s5_w_in=s5_w_in[0].astype(mx), s5_w_glu=s5_w_glu[0].astype(mx),
        s5_tabs=_s5_tables(s5_lambda_re[0], s5_lambda_im[0], s5_log_step[0], s5_b_re[0], s5_b_im[0], s5_c_re[0],
                           s5_c_im[0], s5_d[0]),
        ml_w_in=ml_w_in[0].astype(mx),
        ml_rest=(ml_conv_w[0], ml_conv_b[0], ml_wq[0], ml_wk[0], ml_wv[0], ml_w_gates[0], ml_b_gates[0],
                 ml_norm_g[0], ml_skip[0], ml_w_out[0].astype(mx)),
    )
    return (_trunk(x_prompt, p), _trunk(x_sample, p))
```

```python
import functools

import jax
import jax.numpy as jnp
from jax import lax
from jax.experimental import pallas as pl
from jax.experimental.pallas import tpu as pltpu

F32 = jnp.float32
_MXU_DTYPE = jnp.bfloat16

D_MODEL = 2048
D_FF = 5632
EPS = 1e-6
S5_GROUP = 16
S5_GROUPS = D_MODEL // S5_GROUP
S5_STATE = 64
S5_CHUNK = 16
S5_ROW = S5_CHUNK * S5_GROUP
S5_ST2 = 2 * S5_STATE
S5_GL = 8
S5_PANEL = 4
ML_INNER = 2 * D_MODEL
ML_HEADS = 16
ML_DH = ML_INNER // ML_HEADS
ML_BLK = 4
ML_CONV = 5
ML_CHUNK = 128
ML_NGATE = 4 * ML_HEADS
ML_HP = 4
ML_OG = 4
ML_AUG = 128
HALO = 16

_VMEM_LIMIT = 52 * 1024 * 1024


def _cparams(sem):
    return pltpu.CompilerParams(dimension_semantics=sem, vmem_limit_bytes=_VMEM_LIMIT)


def _mm(a, b):
    return jnp.dot(a, b, preferred_element_type=F32)


def _mm_exact(a, b):
    return jnp.dot(a, b, preferred_element_type=F32, precision=lax.Precision.HIGHEST)


def _rms(x, g):
    ms = jnp.mean(x * x, axis=-1, keepdims=True)
    return x * lax.rsqrt(ms + EPS) * g


def _pick(n, pref):
    t = min(n, pref)
    while n % t:
        t //= 2
    return t


def _ffn_body(x_ref, g_ref, wg_ref, wu_ref, wo_ref, fg_ref, o_ref, hn_ref, *, n_j, final_norm):
    j = pl.program_id(1)

    @pl.when(j == 0)
    def _():
        hn_ref[...] = _rms(x_ref[...], g_ref[...]).astype(hn_ref.dtype)
        o_ref[...] = jnp.zeros_like(o_ref)

    h = hn_ref[...]
    gate = _mm(h, wg_ref[...])
    up = _mm(h, wu_ref[...])
    act = (gate * jax.nn.sigmoid(gate) * up).astype(h.dtype)
    o_ref[...] += _mm(act, wo_ref[...])

    @pl.when(j == n_j - 1)
    def _():
        y = x_ref[...] + 0.5 * o_ref[...]
        if final_norm:
            y = _rms(y, fg_ref[...])
        o_ref[...] = y


def _ffn(x, g, w_in, w_out, k, final_g=None):
    t, d = x.shape
    tm = _pick(t, 512)
    tn = 512
    n_j = D_FF // tn
    fg = g if final_g is None else final_g
    body = functools.partial(_ffn_body, n_j=n_j, final_norm=final_g is not None)
    return pl.pallas_call(
        body,
        grid=(t // tm, n_j),
        in_specs=[
            pl.BlockSpec((tm, d), lambda i, j: (i, 0)),
            pl.BlockSpec((1, d), lambda i, j: (0, 0)),
            pl.BlockSpec((None, d, tn), lambda i, j: (k, 0, j)),
            pl.BlockSpec((None, d, tn), lambda i, j: (k, 0, j + n_j)),
            pl.BlockSpec((None, tn, d), lambda i, j: (k, j, 0)),
            pl.BlockSpec((1, d), lambda i, j: (0, 0)),
        ],
        out_specs=pl.BlockSpec((tm, d), lambda i, j: (i, 0)),
        out_shape=jax.ShapeDtypeStruct((t, d), F32),
        scratch_shapes=[pltpu.VMEM((tm, d), _MXU_DTYPE)],
        compiler_params=_cparams(("parallel", "arbitrary")),
        name="ffn",
    )(x, g.reshape(1, d), w_in, w_in, w_out, fg.reshape(1, d))


def _rms_mm_body(x_ref, g_ref, w_ref, o_ref, hn_ref):
    @pl.when(pl.program_id(1) == 0)
    def _():
        hn_ref[...] = _rms(x_ref[...], g_ref[...]).astype(hn_ref.dtype)

    o_ref[...] = _mm(hn_ref[...], w_ref[...]).astype(o_ref.dtype)


def _rms_mm(x, g, w, name, out_dtype=None):
    t, d = x.shape
    n = w.shape[1]
    tm = _pick(t, 1024)
    tn = 1024
    return pl.pallas_call(
        _rms_mm_body,
        grid=(t // tm, n // tn),
        in_specs=[
            pl.BlockSpec((tm, d), lambda i, j: (i, 0)),
            pl.BlockSpec((1, d), lambda i, j: (0, 0)),
            pl.BlockSpec((d, tn), lambda i, j: (0, j)),
        ],
        out_specs=pl.BlockSpec((tm, tn), lambda i, j: (i, j)),
        out_shape=jax.ShapeDtypeStruct((t, n), out_dtype or _MXU_DTYPE),
        scratch_shapes=[pltpu.VMEM((tm, d), _MXU_DTYPE)],
        compiler_params=_cparams(("parallel", "arbitrary")),
        name=name,
    )(x, g.reshape(1, d), w)


def _s5_kern_body(cq_ref, pd_ref, o_ref):
    for n in range(cq_ref.shape[0]):
        o_ref[n] = _mm_exact(cq_ref[n], pd_ref[n])


def _s5_tables(lam_re, lam_im, log_step, b_re, b_im, c_re, c_im, d):
    g_n, p_n, gs, ck = S5_GROUPS, S5_STATE, S5_GROUP, S5_CHUNK
    lam = lax.complex(jnp.minimum(lam_re.astype(F32), -1e-4), lam_im.astype(F32))
    delta = jnp.exp(log_step.astype(F32))[..., None]
    lam_bar = jnp.exp(lam * delta)
    b_bar = ((lam_bar - 1.0) / lam)[..., None] * lax.complex(b_re.astype(F32), b_im.astype(F32))
    c = lax.complex(c_re.astype(F32), c_im.astype(F32))
    pows = [jnp.ones_like(lam_bar)]
    for _ in range(ck):
        pows.append(pows[-1] * lam_bar)
    pw = jnp.stack(pows)
    pw_rev = jnp.stack(pows[::-1])

    x = pw[:ck, :, :, :, None] * b_bar[None]
    xt = x.transpose(1, 2, 3, 0, 4).reshape(2, g_n, p_n, ck * gs)
    pd_t = jnp.concatenate([jnp.real(xt), jnp.imag(xt)], axis=2).reshape(2 * g_n, 2 * p_n, ck * gs)
    cq_t = jnp.concatenate([jnp.real(c), -jnp.imag(c)], axis=-1).reshape(2 * g_n, gs, 2 * p_n)
    gb = 8
    kt = pl.pallas_call(
        _s5_kern_body,
        grid=(2 * g_n // gb,),
        in_specs=[
            pl.BlockSpec((gb, gs, 2 * p_n), lambda i: (i, 0, 0)),
            pl.BlockSpec((gb, 2 * p_n, ck * gs), lambda i: (i, 0, 0)),
        ],
        out_specs=pl.BlockSpec((gb, gs, ck * gs), lambda i: (i, 0, 0)),
        out_shape=jax.ShapeDtypeStruct((2 * g_n, gs, ck * gs), F32),
        compiler_params=_cparams(("parallel",)),
        name="s5_kernels",
    )(cq_t, pd_t)
    kt = kt.reshape(2, g_n, gs, ck, gs)
    s_i = jnp.arange(ck)[:, None]
    t_i = jnp.arange(ck)[None, :]
    dfw = t_i - s_i
    dbw = s_i - t_i
    tf = jnp.where((dfw >= 0)[None, None, :, :, None], kt[0][:, :, jnp.clip(dfw, 0, ck - 1), :], 0.0)
    tb = jnp.where((dbw >= 0)[None, None, :, :, None], kt[1][:, :, jnp.clip(dbw, 0, ck - 1), :], 0.0)
    toep = (tf + tb).transpose(0, 2, 4, 3, 1).reshape(g_n, ck * gs, ck * gs)

    def reim_last(z):
        return jnp.concatenate([jnp.real(z), jnp.imag(z)], axis=-1)

    xf = pw_rev[1:ck + 1, 0, :, :, None] * b_bar[0][None]
    xf = xf.transpose(1, 0, 3, 2).reshape(g_n, ck * gs, p_n)
    xb = x[:, 1].transpose(1, 0, 3, 2).reshape(g_n, ck * gs, p_n)
    w_state = jnp.concatenate([reim_last(xf), reim_last(xb)], axis=-1)

    zf = (c[0][None] * pw[1:ck + 1, 0][:, :, None, :]).transpose(1, 3, 0, 2).reshape(g_n, p_n, ck * gs)
    zb = (c[1][None] * pw_rev[:ck, 1][:, :, None, :]).transpose(1, 3, 0, 2).reshape(g_n, p_n, ck * gs)
    wo_f = jnp.concatenate([jnp.real(zf), -jnp.imag(zf)], axis=1)
    wo_b = jnp.concatenate([jnp.real(zb), -jnp.imag(zb)], axis=1)

    def coef(a):
        return (jnp.concatenate([jnp.real(a), jnp.real(a)], -1), jnp.concatenate([-jnp.imag(a), jnp.imag(a)], -1))

    a_f = coef(pw[ck, 0])
    a_b = coef(pw[ck, 1])
    mx = _MXU_DTYPE
    n8, gl = g_n // S5_GL, S5_GL
    kk = ck * gl * gs

    def rows_sgc(w):
        n = w.shape[-1]
        return w.astype(mx).reshape(n8, gl, ck, gs, n).transpose(0, 2, 1, 3, 4).reshape(n8, kk, n)

    def rows_gp(w):
        return w.astype(mx).reshape(n8, gl * w.shape[1], w.shape[2])

    td = (ck, gs)
    ep = (2, 2 * p_n)
    return dict(toep=_s5_widen(rows_sgc(toep), gs, td), w_state=_s5_widen(rows_sgc(w_state), gs, ep),
                wo_f=_s5_widen(rows_gp(wo_f), 2 * p_n, td), wo_b=_s5_widen(rows_gp(wo_b), 2 * p_n, td),
                a_f=a_f, a_b=a_b, d=d.astype(F32).reshape(1, g_n * gs))


def _s5_widen_body(w_ref, e_ref, o_ref, *, row_div, col_div):
    y = _mm(w_ref[0], e_ref[...])
    r_g = (lax.broadcasted_iota(jnp.int32, y.shape, 0) // row_div) % S5_GL
    c_g = ((lax.broadcasted_iota(jnp.int32, y.shape, 1) + pl.program_id(1) * y.shape[1]) // col_div) % S5_GL
    o_ref[0] = jnp.where(r_g == c_g, y, 0.0).astype(o_ref.dtype)


def _s5_widen(w, row_div, col_dims):
    n8, rows, n = w.shape
    a, b = col_dims
    e = jnp.broadcast_to(jnp.eye(a * b, dtype=w.dtype).reshape(a, b, a, 1, b), (a, b, a, S5_GL, b))
    e = e.reshape(n, n * S5_GL)
    tn = 512
    return pl.pallas_call(
        functools.partial(_s5_widen_body, row_div=row_div, col_div=b),
        grid=(n8, n * S5_GL // tn),
        in_specs=[pl.BlockSpec((1, rows, n), lambda x, j: (x, 0, 0)), pl.BlockSpec((n, tn), lambda x, j: (0, j))],
        out_specs=pl.BlockSpec((1, rows, tn), lambda x, j: (x, 0, j)),
        out_shape=jax.ShapeDtypeStruct((n8, rows, n * S5_GL), w.dtype),
        compiler_params=_cparams(("parallel", "arbitrary")),
        name="s5_widen",
    )(w, e)


def _s5_gather_chunks(x_ref, xc_ref):
    rt = xc_ref.shape[0]
    parts = [x_ref[pl.ds(s, rt, stride=S5_CHUNK), :] for s in range(S5_CHUNK)]
    xc_ref[...] = jnp.concatenate(parts, axis=1).astype(xc_ref.dtype)


def _s5_state_in_body(u_ref, w_ref, vf_ref, vb_ref, xc_ref, pf_ref, pb_ref):
    _s5_gather_chunks(u_ref, xc_ref)
    rt = xc_ref.shape[0]
    v = _mm(xc_ref[...], w_ref[0])
    for n, (o_ref, p_ref) in enumerate(((vf_ref, pf_ref), (vb_ref, pb_ref))):
        for gi in range(S5_GL):
            c0 = (n * S5_GL + gi) * S5_ST2
            p_ref[pl.ds(gi, rt, stride=S5_GL), :] = v[:, c0:c0 + S5_ST2]
        o_ref[...] = p_ref[...].reshape(o_ref.shape)


def _s5_scan_body(vf_ref, vb_ref, afr_ref, afi_ref, abr_ref, abi_ref, sf_ref, sb_ref, cf_ref, cb_ref, *, rt):
    @pl.when(pl.program_id(1) == 0)
    def _():
        cf_ref[...] = jnp.zeros_like(cf_ref)
        cb_ref[...] = jnp.zeros_like(cb_ref)

    afr, afi, abr, abi = afr_ref[...], afi_ref[...], abr_ref[...], abi_ref[...]

    def step(k, carry):
        s_f, s_b = carry
        sf_ref[0, k] = s_f.astype(sf_ref.dtype)
        s_f = afr * s_f + afi * pltpu.roll(s_f, S5_STATE, 1) + vf_ref[0, k]
        kb = rt - 1 - k
        sb_ref[0, kb] = s_b.astype(sb_ref.dtype)
        s_b = abr * s_b + abi * pltpu.roll(s_b, S5_STATE, 1) + vb_ref[0, kb]
        return s_f, s_b

    s_f, s_b = lax.fori_loop(0, rt, step, (cf_ref[...], cb_ref[...]))
    cf_ref[...] = s_f
    cb_ref[...] = s_b


def _s5_out_body(u_ref, t_ref, sf_ref, sb_ref, wof_ref, wob_ref, d_ref, o_ref, xc_ref, pf_ref, pb_ref):
    _s5_gather_chunks(u_ref, xc_ref)
    rt = xc_ref.shape[0]
    mxd = xc_ref.dtype

    def by_group(s_ref, p_ref):
        p_ref[...] = s_ref[...].reshape(p_ref.shape)
        return jnp.concatenate([p_ref[pl.ds(gi, rt, stride=S5_GL), :] for gi in range(S5_GL)], axis=1).astype(mxd)

    s_f = by_group(sf_ref, pf_ref)
    s_b = by_group(sb_ref, pb_ref)
    xc = xc_ref[...]
    pw = S5_PANEL * 128
    for p0 in range(0, S5_CHUNK, S5_PANEL):
        cols = pl.ds(p0 * 128, pw)
        y = _mm(xc, t_ref[0, :, cols]) + _mm(s_f, wof_ref[0, :, cols]) + _mm(s_b, wob_ref[0, :, cols])
        for s in range(S5_PANEL):
            rows = pl.ds(p0 + s, rt, stride=S5_CHUNK)
            o_ref[rows, :] = jax.nn.gelu(y[:, s * 128:(s + 1) * 128] + d_ref[...] * u_ref[rows, :])


def _glu_body(y_ref, wv_ref, wg_ref, x_ref, o_ref, yb_ref):
    @pl.when(pl.program_id(1) == 0)
    def _():
        yb_ref[...] = y_ref[...].astype(yb_ref.dtype)

    y = yb_ref[...]
    val = _mm(y, wv_ref[...])
    gate = _mm(y, wg_ref[...])
    o_ref[...] = x_ref[...] + val * jax.nn.sigmoid(gate)


def _s5_mixer(x, bsz, length, g, w_in, tabs, w_glu):
    t, d = x.shape
    g_n, ck = S5_GROUPS, S5_CHUNK
    r_n = length // ck
    rows = r_n * bsz
    mx = _MXU_DTYPE
    n8 = g_n // S5_GL
    lanes = S5_GL * S5_GROUP
    k8 = ck * lanes
    st8 = S5_GL * S5_ST2
    u = _rms_mm(x, g, w_in, "s5_in", out_dtype=F32)

    rb = _pick(rows, 256)
    u_spec = pl.BlockSpec((rb * ck, lanes), lambda j, i: (i, j))
    st_spec = lambda: pl.BlockSpec((rb, S5_GL, S5_ST2), lambda j, i: (i, j, 0))
    held = lambda shape: pl.BlockSpec((1,) + shape, lambda j, i: (j, 0, 0))
    vf, vb = pl.pallas_call(
        _s5_state_in_body,
        grid=(n8, rows // rb),
        in_specs=[u_spec, held((k8, 2 * st8))],
        out_specs=[st_spec(), st_spec()],
        out_shape=[jax.ShapeDtypeStruct((rows, g_n, S5_ST2), F32)] * 2,
        scratch_shapes=[pltpu.VMEM((rb, k8), mx)] + [pltpu.VMEM((rb * S5_GL, S5_ST2), F32)] * 2,
        compiler_params=_cparams(("parallel", "arbitrary")),
        name="s5_state_in",
    )(u, tabs["w_state"])

    rt = _pick(r_n, 16)
    n_rt = r_n // rt
    v4 = (bsz, r_n, g_n, S5_ST2)
    blk = (1, rt, g_n, S5_ST2)
    fwd = lambda b, i: (b, i, 0, 0)
    bwd = lambda b, i: (b, n_rt - 1 - i, 0, 0)
    coef_spec = pl.BlockSpec((g_n, S5_ST2), lambda b, i: (0, 0))
    sf, sb = pl.pallas_call(
        functools.partial(_s5_scan_body, rt=rt),
        grid=(bsz, n_rt),
        in_specs=[pl.BlockSpec(blk, fwd), pl.BlockSpec(blk, bwd), coef_spec, coef_spec, coef_spec, coef_spec],
        out_specs=[pl.BlockSpec(blk, fwd), pl.BlockSpec(blk, bwd)],
        out_shape=[jax.ShapeDtypeStruct(v4, F32)] * 2,
        scratch_shapes=[pltpu.VMEM((g_n, S5_ST2), F32)] * 2,
        compiler_params=_cparams(("parallel", "arbitrary")),
        name="s5_scan",
    )(vf.reshape(v4), vb.reshape(v4), tabs["a_f"][0], tabs["a_f"][1], tabs["a_b"][0], tabs["a_b"][1])
    sf = sf.reshape(rows, g_n, S5_ST2)
    sb = sb.reshape(rows, g_n, S5_ST2)

    y = pl.pallas_call(
        _s5_out_body,
        grid=(n8, rows // rb),
        in_specs=[u_spec, held((k8, k8)), st_spec(), st_spec(), held((st8, k8)), held((st8, k8)),
                  pl.BlockSpec((1, lanes), lambda j, i: (0, j))],
        out_specs=u_spec,
        out_shape=jax.ShapeDtypeStruct((t, d), F32),
        scratch_shapes=[pltpu.VMEM((rb, k8), mx)] + [pltpu.VMEM((rb * S5_GL, S5_ST2), F32)] * 2,
        compiler_params=_cparams(("parallel", "arbitrary")),
        name="s5_out",
    )(u, tabs["toep"], sf, sb, tabs["wo_f"], tabs["wo_b"], tabs["d"])

    tm = _pick(t, 1024)
    tn = 512
    n_j = d // tn
    return pl.pallas_call(
        _glu_body,
        grid=(t // tm, n_j),
        in_specs=[
            pl.BlockSpec((tm, d), lambda i, j: (i, 0)),
            pl.BlockSpec((d, tn), lambda i, j: (0, j)),
            pl.BlockSpec((d, tn), lambda i, j: (0, j + n_j)),
            pl.BlockSpec((tm, tn), lambda i, j: (i, j)),
        ],
        out_specs=pl.BlockSpec((tm, tn), lambda i, j: (i, j)),
        out_shape=jax.ShapeDtypeStruct((t, d), F32),
        scratch_shapes=[pltpu.VMEM((tm, d), mx)],
        compiler_params=_cparams(("parallel", "arbitrary")),
        name="s5_glu",
    )(y, w_glu, w_glu, x)


def _log_sigmoid(x):
    return jnp.minimum(x, 0.0) - jnp.log(1.0 + jnp.exp(-jnp.abs(x)))


def _ml_pre_body(xm_ref, prev_ref, next_ref, cw_ref, cb_ref, bdq_ref, bdkt_ref, bdv_ref, wgt_ref, bgt_ref,
                 q_ref, kt_ref, v_ref, xc_ref, gatest_ref, pt_ref, st_ref, ext_ref, *, tiles_per_seq):
    tm = xm_ref.shape[0]
    mxd = xm_ref.dtype
    pos = lax.rem(pl.program_id(0), tiles_per_seq)
    ext_ref[pl.ds(HALO, tm), :] = xm_ref[...].astype(F32)
    ext_ref[pl.ds(0, HALO), :] = jnp.where(pos == 0, 0.0, prev_ref[...].astype(F32))
    ext_ref[pl.ds(HALO + tm, HALO), :] = jnp.where(pos == tiles_per_seq - 1, 0.0, next_ref[...].astype(F32))
    nt = (((1,), (1,)), ((), ()))
    scale = ML_DH ** -0.5

    def conv(h):
        sl = pl.ds(h * ML_DH, ML_DH)
        pre = cb_ref[:, sl]
        for jj in range(ML_CONV):
            pre = pre + cw_ref[jj:jj + 1, sl] * ext_ref[pl.ds(HALO - ML_CONV // 2 + jj, tm), sl]
        xc = (pre * jax.nn.sigmoid(pre)).astype(mxd)
        xc_ref[:, sl] = xc
        return xc

    def project(h, xc):
        sl = pl.ds(h * ML_DH, ML_DH)
        qh = _mm(xc, bdq_ref[h]).astype(mxd)
        vh = _mm(xm_ref[:, sl], bdv_ref[h]).astype(mxd)
        kt = (lax.dot_general(bdkt_ref[h], xc, nt, preferred_element_type=F32) * scale).astype(mxd)
        q_ref[:, sl] = qh
        v_ref[:, sl] = vh
        for c in range(tm // ML_CHUNK):
            kt_ref[c, sl, :] = kt[:, c * ML_CHUNK:(c + 1) * ML_CHUNK]
        return (lax.dot_general(wgt_ref[0, :, sl], qh, nt, preferred_element_type=F32)
                + _mm(wgt_ref[1, :, sl], kt) * (1.0 / scale)
                + lax.dot_general(wgt_ref[2, :, sl], vh, nt, preferred_element_type=F32))

    gates_t = jnp.broadcast_to(bgt_ref[...], (ML_NGATE, tm))
    xc_prev = conv(0)
    for h in range(1, ML_HEADS):
        xc_next = conv(h)
        gates_t = gates_t + project(h - 1, xc_prev)
        xc_prev = xc_next
    gates_t = gates_t + project(ML_HEADS - 1, xc_prev)

    r_i = lax.broadcasted_iota(jnp.int32, (ML_CHUNK, ML_CHUNK), 0)
    c_i = lax.broadcasted_iota(jnp.int32, (ML_CHUNK, ML_CHUNK), 1)
    incl_before = (r_i <= c_i).astype(F32)
    incl_after = (r_i >= c_i).astype(F32)
    gatest_ref[...] = gates_t
    for c in range(tm // ML_CHUNK):
        sl = pl.ds(c * ML_CHUNK, ML_CHUNK)
        lst = _log_sigmoid(gates_t[:, c * ML_CHUNK:(c + 1) * ML_CHUNK])
        pt_ref[:, sl] = _mm_exact(lst, incl_before)
        st_ref[:, sl] = _mm_exact(lst, incl_after)


def _mlstm_body(qf, kf, vf, rowf, qb, kb, vb, rowb, hf_ref, hb_ref, m_ref, *c_refs):
    @pl.when(pl.program_id(2) == 0)
    def _():
        m_ref[...] = jnp.zeros_like(m_ref)
        for c_ref in c_refs:
            c_ref[...] = jnp.zeros_like(c_ref)

    io = ((qf, kf, vf, rowf, hf_ref), (qb, kb, vb, rowb, hb_ref))
    chains = [(hd, d) for hd in range(ML_HP) for d in (0, 1)]
    t_i = lax.broadcasted_iota(jnp.int32, (ML_CHUNK, ML_CHUNK), 0)
    s_i = lax.broadcasted_iota(jnp.int32, (ML_CHUNK, ML_CHUNK), 1)
    masks = (s_i <= t_i, s_i >= t_i)
    m_all = m_ref[...]
    mxd = qf.dtype
    sq = (ML_CHUNK, ML_AUG)
    ones_blk = jnp.ones(sq, mxd)

    st = []
    for i, (hd, d) in enumerate(chains):
        row = io[d][3][hd]
        col = jnp.concatenate([row, row], axis=0).T
        bc = col[:, d:d + 1]
        lic = col[:, 2 + d:3 + d]
        bcr = row[d:d + 1, :]
        lir = row[2 + d:3 + d, :]
        g = bcr[:, ML_CHUNK - 1:ML_CHUNK] if d == 0 else bcr[:, 0:1]
        m_prev = m_all[i:i + 1, 0:1]
        bc_b = jnp.broadcast_to(bc, sq)
        a_b = bc_b + m_prev
        drow = lir - bcr
        dm = jnp.where(masks[d], bc_b + drow, -jnp.inf)
        m_t = jnp.maximum(a_b, jnp.max(dm, axis=1, keepdims=True))
        dec = jnp.exp(dm - m_t)
        e = jnp.exp(a_b - m_t)
        m_new = jnp.maximum(g + m_prev, jnp.max(g + drow, axis=1, keepdims=True))
        decay = jnp.exp(g + m_prev - m_new)
        wr = jnp.exp(g - bc_b + jnp.broadcast_to(lic, sq) - m_new)
        st.append(dict(m_t=m_t, dec=dec, e=e, m_new=m_new, decay=decay, wr=wr))

    for i, (hd, d) in enumerate(chains):
        sl = pl.ds(hd * ML_DH, ML_DH)
        s = _mm(io[d][0][:, sl], io[d][1][sl, :]) * st[i]["dec"]
        st[i]["s"] = s.astype(mxd)

    for i, (hd, d) in enumerate(chains):
        sl = pl.ds(hd * ML_DH, ML_DH)
        v_aug = jnp.concatenate([io[d][2][:, sl], ones_blk], axis=1)
        e3 = jnp.concatenate([st[i]["e"]] * (ML_DH // ML_AUG + 1), axis=1)
        nd = e3 * _mm(io[d][0][:, sl], c_refs[i][...].astype(mxd)) + _mm(st[i]["s"], v_aug)
        inv = 1.0 / jnp.maximum(jnp.abs(nd[:, ML_DH:]), jnp.exp(-st[i]["m_t"]))
        inv2 = jnp.concatenate([inv] * (ML_DH // ML_AUG), axis=1)
        io[d][4][:, sl] = (nd[:, :ML_DH] * inv2).astype(io[d][4].dtype)

    for i, (hd, d) in enumerate(chains):
        sl = pl.ds(hd * ML_DH, ML_DH)
        v_aug = jnp.concatenate([io[d][2][:, sl], ones_blk], axis=1)
        wr3 = jnp.concatenate([st[i]["wr"]] * (ML_DH // ML_AUG + 1), axis=1)
        vw = (v_aug.astype(F32) * wr3).astype(mxd)
        c_refs[i][...] = st[i]["decay"] * c_refs[i][...] + _mm(io[d][1][sl, :], vw)
    m_ref[...] = jnp.concatenate([jnp.broadcast_to(c["m_new"], (1, m_all.shape[1])) for c in st], axis=0)


def _ml_out_body(hf_ref, hb_ref, z_ref, xc_ref, ng_ref, sk_ref, w_ref, x_ref, o_ref):
    @pl.when(pl.program_id(1) == 0)
    def _():
        o_ref[...] = x_ref[...]

    acc = o_ref[...]
    for hd in range(ML_OG):
        sl = pl.ds(hd * ML_DH, ML_DH)
        hh = hf_ref[:, sl].astype(F32) + hb_ref[:, sl].astype(F32)
        mu = jnp.mean(hh, axis=-1, keepdims=True)
        cen = hh - mu
        var = jnp.mean(cen * cen, axis=-1, keepdims=True)
        hn = cen * lax.rsqrt(var + EPS) * ng_ref[:, sl]
        out = jax.nn.sigmoid(z_ref[:, sl].astype(F32)) * (hn + sk_ref[:, sl] * xc_ref[:, sl].astype(F32))
        acc = acc + _mm(out.astype(w_ref.dtype), w_ref[sl, :])
    o_ref[...] = acc


def _blockdiag(w):
    nb = ML_DH // ML_BLK
    wt = w.astype(F32).reshape(ML_HEADS, nb, ML_BLK, ML_BLK)
    eye = jnp.eye(nb, dtype=F32)
    return jnp.einsum("hncd,nm->hncmd", wt, eye).reshape(ML_HEADS, ML_DH, ML_DH).astype(_MXU_DTYPE)


def _mlstm_mixer(x, bsz, length, g, w_in, conv_w, conv_b, wq, wk, wv, w_gates, b_gates, norm_g, skip, w_out):
    t, d = x.shape
    mx = _MXU_DTYPE
    di, dh, nh = ML_INNER, ML_DH, ML_HEADS
    xz = _rms_mm(x, g, w_in, "ml_in")

    tm = _pick(length, 256)
    tiles_per_seq = length // tm
    hb = tm // HALO
    n_h = t // HALO
    wgt = w_gates.astype(mx).transpose(0, 2, 1)
    row_spec = lambda: pl.BlockSpec((tm, di), lambda i: (i, 0))
    small_t = lambda: pl.BlockSpec((ML_NGATE, tm), lambda i: (0, i))
    whole = lambda a: pl.BlockSpec(a.shape, lambda i: (0,) * a.ndim)
    consts = (conv_w.astype(F32), conv_b.astype(F32).reshape(1, di), _blockdiag(wq),
              _blockdiag(wk).transpose(0, 2, 1), _blockdiag(wv), wgt, b_gates.astype(F32).reshape(ML_NGATE, 1))
    outs = pl.pallas_call(
        functools.partial(_ml_pre_body, tiles_per_seq=tiles_per_seq),
        grid=(t // tm,),
        in_specs=[
            row_spec(),
            pl.BlockSpec((HALO, di), lambda i: (jnp.maximum(i * hb - 1, 0), 0)),
            pl.BlockSpec((HALO, di), lambda i: (jnp.minimum((i + 1) * hb, n_h - 1), 0)),
        ] + [whole(a) for a in consts],
        out_specs=[row_spec(), pl.BlockSpec((tm // ML_CHUNK, di, ML_CHUNK), lambda i: (i, 0, 0)), row_spec(),
                   row_spec(), small_t(), small_t(), small_t()],
        out_shape=[jax.ShapeDtypeStruct((t, di), mx), jax.ShapeDtypeStruct((t // ML_CHUNK, di, ML_CHUNK), mx)]
        + [jax.ShapeDtypeStruct((t, di), mx)] * 2 + [jax.ShapeDtypeStruct((ML_NGATE, t), F32)] * 3,
        scratch_shapes=[pltpu.VMEM((tm + 2 * HALO, di), F32)],
        compiler_params=_cparams(("parallel",)),
        name="ml_pre",
    )(xz, xz, xz, *consts)
    q, kt, v, xc, gates_t, pre_t, suf_t = outs
    rows = jnp.stack([pre_t[nh:2 * nh], suf_t[3 * nh:], gates_t[:nh], gates_t[2 * nh:3 * nh]], axis=1)

    nc = length // ML_CHUNK
    hp = ML_HP
    fw = lambda b, h, j: (b * nc + j, h)
    bw = lambda b, h, j: (b * nc + nc - 1 - j, h)
    qkv = lambda f: pl.BlockSpec((ML_CHUNK, hp * dh), f)
    ktspec = lambda f: pl.BlockSpec((None, hp * dh, ML_CHUNK), lambda b, h, j: f(b, h, j) + (0,))
    rowspec = lambda f: pl.BlockSpec((hp, 4, ML_CHUNK), lambda b, h, j: (h, 0, f(b, h, j)[0]))
    hf, hbk = pl.pallas_call(
        _mlstm_body,
        grid=(bsz, nh // hp, nc),
        in_specs=[qkv(fw), ktspec(fw), qkv(fw), rowspec(fw), qkv(bw), ktspec(bw), qkv(bw), rowspec(bw)],
        out_specs=[qkv(fw), qkv(bw)],
        out_shape=[jax.ShapeDtypeStruct((t, di), mx)] * 2,
        scratch_shapes=[pltpu.VMEM((2 * hp, 128), F32)] + [pltpu.VMEM((dh, dh + ML_AUG), F32)] * (2 * hp),
        compiler_params=_cparams(("parallel", "parallel", "arbitrary")),
        name="mlstm",
    )(q, kt, v, rows, q, kt, v, rows)

    tmo = _pick(t, 512)
    kg = ML_OG * dh
    n_g = nh // ML_OG
    grp = lambda: pl.BlockSpec((tmo, kg), lambda i, j: (i, j))
    return pl.pallas_call(
        _ml_out_body,
        grid=(t // tmo, n_g),
        in_specs=[
            grp(), grp(),
            pl.BlockSpec((tmo, kg), lambda i, j: (i, j + n_g)),
            grp(),
            pl.BlockSpec((1, kg), lambda i, j: (0, j)),
            pl.BlockSpec((1, kg), lambda i, j: (0, j)),
            pl.BlockSpec((kg, d), lambda i, j: (j, 0)),
            pl.BlockSpec((tmo, d), lambda i, j: (i, 0)),
        ],
        out_specs=pl.BlockSpec((tmo, d), lambda i, j: (i, 0)),
        out_shape=jax.ShapeDtypeStruct((t, d), F32),
        compiler_params=_cparams(("parallel", "arbitrary")),
        name="ml_out",
    )(hf, hbk, xz, xc, norm_g.astype(F32).reshape(1, di), skip.astype(F32).reshape(1, di), w_out, x)


def _trunk(x3, p):
    bsz, length, d = x3.shape
    x = x3.reshape(bsz * length, d)
    ffn = lambda x, layer, idx, **kw: _ffn(x, p["norm_g"][layer, 2 * idx], p["ffn_w_in"], p["ffn_w_out"],
                                           2 * layer + idx, **kw)
    x = ffn(x, 0, 0)
    x = _s5_mixer(x, bsz, length, p["norm_g"][0, 1], p["s5_w_in"], p["s5_tabs"], p["s5_w_glu"])
    x = ffn(x, 0, 1)
    x = ffn(x, 1, 0)
    x = _mlstm_mixer(x, bsz, length, p["norm_g"][1, 1], p["ml_w_in"], *p["ml_rest"])
    x = ffn(x, 1, 1, final_g=p["final_g"])
    return x.reshape(bsz, length, d)


def kernel(x_prompt, x_sample, norm_g, final_g, ffn_w_in, ffn_w_out, s5_w_in, s5_lambda_re, s5_lambda_im, s5_log_step, s5_b_re, s5_b_im, s5_c_re, s5_c_im, s5_d, s5_w_glu, ml_w_in, ml_conv_w, ml_conv_b, ml_wq, ml_wk, ml_wv, ml_w_gates, ml_b_gates, ml_norm_g, ml_skip, ml_w_out):
    mx = _MXU_DTYPE
    p = dict(
        norm_g=norm_g.astype(F32), final_g=final_g.astype(F32),
        ffn_w_in=ffn_w_in.astype(mx).reshape(-1, D_MODEL, 2 * D_FF),
        ffn_w_out=ffn_w_out.astype(mx).reshape(-1, D_FF, D_MODEL),
        s5_w_in=s5_w_in[0].astype(mx), s5_w_glu=s5_w_glu[0].astype(mx),
        s5_tabs=_s5_tables(s5_lambda_re[0], s5_lambda_im[0], s5_log_step[0], s5_b_re[0], s5_b_im[0], s5_c_re[0],
                           s5_c_im[0], s5_d[0]),
        ml_w_in=ml_w_in[0].astype(mx),
        ml_rest=(ml_conv_w[0], ml_conv_b[0], ml_wq[0], ml_wk[0], ml_wv[0], ml_w_gates[0], ml_b_gates[0],
                 ml_norm_g[0], ml_skip[0], ml_w_out[0].astype(mx)),
    )
    return (_trunk(x_prompt, p), _trunk(x_sample, p))
```

```python
import functools

import jax
import jax.numpy as jnp
from jax import lax
from jax.experimental import pallas as pl
from jax.experimental.pallas import tpu as pltpu

F32 = jnp.float32
_MXU_DTYPE = jnp.bfloat16

D_MODEL = 2048
D_FF = 5632
EPS = 1e-6
S5_GROUP = 16
S5_GROUPS = D_MODEL // S5_GROUP
S5_STATE = 64
S5_CHUNK = 16
S5_ROW = S5_CHUNK * S5_GROUP
S5_ST2 = 2 * S5_STATE
S5_GL = 8
S5_PANEL = 4
ML_INNER = 2 * D_MODEL
ML_HEADS = 16
ML_DH = ML_INNER // ML_HEADS
ML_BLK = 4
ML_CONV = 5
ML_CHUNK = 128
ML_NGATE = 4 * ML_HEADS
ML_HP = 4
ML_OG = 4
ML_AUG = 128
HALO = 16

_VMEM_LIMIT = 52 * 1024 * 1024
_VMEM_LIMIT_FFN = 60 * 1024 * 1024


def _cparams(sem, limit=_VMEM_LIMIT):
    return pltpu.CompilerParams(dimension_semantics=sem, vmem_limit_bytes=limit)


def _mm(a, b):
    return jnp.dot(a, b, preferred_element_type=F32)


def _mm_exact(a, b):
    return jnp.dot(a, b, preferred_element_type=F32, precision=lax.Precision.HIGHEST)


def _rms(x, g):
    ms = jnp.mean(x * x, axis=-1, keepdims=True)
    return x * lax.rsqrt(ms + EPS) * g


def _pick(n, pref):
    t = min(n, pref)
    while n % t:
        t //= 2
    return t


def _ffn_body(x_ref, g_ref, wg_ref, wu_ref, wo_ref, fg_ref, o_ref, hn_ref, *, n_j, final_norm):
    j = pl.program_id(1)

    @pl.when(j == 0)
    def _():
        hn_ref[...] = _rms(x_ref[...], g_ref[...]).astype(hn_ref.dtype)
        o_ref[...] = jnp.zeros_like(o_ref)

    h = hn_ref[...]
    gate = _mm(h, wg_ref[...])
    up = _mm(h, wu_ref[...])
    act = (gate * jax.nn.sigmoid(gate) * up).astype(h.dtype)
    o_ref[...] += _mm(act, wo_ref[...])

    @pl.when(j == n_j - 1)
    def _():
        y = x_ref[...] + 0.5 * o_ref[...]
        if final_norm:
            y = _rms(y, fg_ref[...])
        o_ref[...] = y


def _ffn(x, g, w_in, w_out, k, final_g=None):
    t, d = x.shape
    tm = _pick(t, 1024 if final_g is None else 512)
    tn = 512
    n_j = D_FF // tn
    fg = g if final_g is None else final_g
    body = functools.partial(_ffn_body, n_j=n_j, final_norm=final_g is not None)
    return pl.pallas_call(
        body,
        grid=(t // tm, n_j),
        in_specs=[
            pl.BlockSpec((tm, d), lambda i, j: (i, 0)),
            pl.BlockSpec((1, d), lambda i, j: (0, 0)),
            pl.BlockSpec((None, d, tn), lambda i, j: (k, 0, j)),
            pl.BlockSpec((None, d, tn), lambda i, j: (k, 0, j + n_j)),
            pl.BlockSpec((None, tn, d), lambda i, j: (k, j, 0)),
            pl.BlockSpec((1, d), lambda i, j: (0, 0)),
        ],
        out_specs=pl.BlockSpec((tm, d), lambda i, j: (i, 0)),
        out_shape=jax.ShapeDtypeStruct((t, d), F32),
        scratch_shapes=[pltpu.VMEM((tm, d), _MXU_DTYPE)],
        compiler_params=_cparams(("parallel", "arbitrary"), _VMEM_LIMIT_FFN),
        name="ffn",
    )(x, g.reshape(1, d), w_in, w_in, w_out, fg.reshape(1, d))


def _rms_mm_body(x_ref, g_ref, w_ref, o_ref, hn_ref):
    @pl.when(pl.program_id(1) == 0)
    def _():
        hn_ref[...] = _rms(x_ref[...], g_ref[...]).astype(hn_ref.dtype)

    o_ref[...] = _mm(hn_ref[...], w_ref[...]).astype(o_ref.dtype)


def _rms_mm(x, g, w, name, out_dtype=None):
    t, d = x.shape
    n = w.shape[1]
    tm = _pick(t, 1024)
    tn = 1024
    return pl.pallas_call(
        _rms_mm_body,
        grid=(t // tm, n // tn),
        in_specs=[
            pl.BlockSpec((tm, d), lambda i, j: (i, 0)),
            pl.BlockSpec((1, d), lambda i, j: (0, 0)),
            pl.BlockSpec((d, tn), lambda i, j: (0, j)),
        ],
        out_specs=pl.BlockSpec((tm, tn), lambda i, j: (i, j)),
        out_shape=jax.ShapeDtypeStruct((t, n), out_dtype or _MXU_DTYPE),
        scratch_shapes=[pltpu.VMEM((tm, d), _MXU_DTYPE)],
        compiler_params=_cparams(("parallel", "arbitrary")),
        name=name,
    )(x, g.reshape(1, d), w)


def _s5_kern_body(cq_ref, pd_ref, o_ref):
    for n in range(cq_ref.shape[0]):
        o_ref[n] = _mm_exact(cq_ref[n], pd_ref[n])


def _s5_tables(lam_re, lam_im, log_step, b_re, b_im, c_re, c_im, d):
    g_n, p_n, gs, ck = S5_GROUPS, S5_STATE, S5_GROUP, S5_CHUNK
    lam = lax.complex(jnp.minimum(lam_re.astype(F32), -1e-4), lam_im.astype(F32))
    delta = jnp.exp(log_step.astype(F32))[..., None]
    lam_bar = jnp.exp(lam * delta)
    b_bar = ((lam_bar - 1.0) / lam)[..., None] * lax.complex(b_re.astype(F32), b_im.astype(F32))
    c = lax.complex(c_re.astype(F32), c_im.astype(F32))
    pows = [jnp.ones_like(lam_bar)]
    for _ in range(ck):
        pows.append(pows[-1] * lam_bar)
    pw = jnp.stack(pows)
    pw_rev = jnp.stack(pows[::-1])

    x = pw[:ck, :, :, :, None] * b_bar[None]
    xt = x.transpose(1, 2, 3, 0, 4).reshape(2, g_n, p_n, ck * gs)
    pd_t = jnp.concatenate([jnp.real(xt), jnp.imag(xt)], axis=2).reshape(2 * g_n, 2 * p_n, ck * gs)
    cq_t = jnp.concatenate([jnp.real(c), -jnp.imag(c)], axis=-1).reshape(2 * g_n, gs, 2 * p_n)
    gb = 8
    kt = pl.pallas_call(
        _s5_kern_body,
        grid=(2 * g_n // gb,),
        in_specs=[
            pl.BlockSpec((gb, gs, 2 * p_n), lambda i: (i, 0, 0)),
            pl.BlockSpec((gb, 2 * p_n, ck * gs), lambda i: (i, 0, 0)),
        ],
        out_specs=pl.BlockSpec((gb, gs, ck * gs), lambda i: (i, 0, 0)),
        out_shape=jax.ShapeDtypeStruct((2 * g_n, gs, ck * gs), F32),
        compiler_params=_cparams(("parallel",)),
        name="s5_kernels",
    )(cq_t, pd_t)
    kt = kt.reshape(2, g_n, gs, ck, gs)
    s_i = jnp.arange(ck)[:, None]
    t_i = jnp.arange(ck)[None, :]
    dfw = t_i - s_i
    dbw = s_i - t_i
    tf = jnp.where((dfw >= 0)[None, None, :, :, None], kt[0][:, :, jnp.clip(dfw, 0, ck - 1), :], 0.0)
    tb = jnp.where((dbw >= 0)[None, None, :, :, None], kt[1][:, :, jnp.clip(dbw, 0, ck - 1), :], 0.0)
    toep = (tf + tb).transpose(0, 2, 4, 3, 1).reshape(g_n, ck * gs, ck * gs)

    def reim_last(z):
        return jnp.concatenate([jnp.real(z), jnp.imag(z)], axis=-1)

    xf = pw_rev[1:ck + 1, 0, :, :, None] * b_bar[0][None]
    xf = xf.transpose(1, 0, 3, 2).reshape(g_n, ck * gs, p_n)
    xb = x[:, 1].transpose(1, 0, 3, 2).reshape(g_n, ck * gs, p_n)
    w_state = jnp.concatenate([reim_last(xf), reim_last(xb)], axis=-1)

    zf = (c[0][None] * pw[1:ck + 1, 0][:, :, None, :]).transpose(1, 3, 0, 2).reshape(g_n, p_n, ck * gs)
    zb = (c[1][None] * pw_rev[:ck, 1][:, :, None, :]).transpose(1, 3, 0, 2).reshape(g_n, p_n, ck * gs)
    wo_f = jnp.concatenate([jnp.real(zf), -jnp.imag(zf)], axis=1)
    wo_b = jnp.concatenate([jnp.real(zb), -jnp.imag(zb)], axis=1)

    def coef(a):
        return (jnp.concatenate([jnp.real(a), jnp.real(a)], -1), jnp.concatenate([-jnp.imag(a), jnp.imag(a)], -1))

    a_f = coef(pw[ck, 0])
    a_b = coef(pw[ck, 1])
    mx = _MXU_DTYPE
    n8, gl = g_n // S5_GL, S5_GL
    kk = ck * gl * gs

    def rows_sgc(w):
        n = w.shape[-1]
        return w.astype(mx).reshape(n8, gl, ck, gs, n).transpose(0, 2, 1, 3, 4).reshape(n8, kk, n)

    def rows_gp(w):
        return w.astype(mx).reshape(n8, gl * w.shape[1], w.shape[2])

    td = (ck, gs)
    ep = (2, 2 * p_n)
    return dict(toep=_s5_widen(rows_sgc(toep), gs, td), w_state=_s5_widen(rows_sgc(w_state), gs, ep),
                wo_f=_s5_widen(rows_gp(wo_f), 2 * p_n, td), wo_b=_s5_widen(rows_gp(wo_b), 2 * p_n, td),
                a_f=a_f, a_b=a_b, d=d.astype(F32).reshape(1, g_n * gs))


def _s5_widen_body(w_ref, e_ref, o_ref, *, row_div, col_div):
    y = _mm(w_ref[0], e_ref[...])
    r_g = (lax.broadcasted_iota(jnp.int32, y.shape, 0) // row_div) % S5_GL
    c_g = ((lax.broadcasted_iota(jnp.int32, y.shape, 1) + pl.program_id(1) * y.shape[1]) // col_div) % S5_GL
    o_ref[0] = jnp.where(r_g == c_g, y, 0.0).astype(o_ref.dtype)


def _s5_widen(w, row_div, col_dims):
    n8, rows, n = w.shape
    a, b = col_dims
    e = jnp.broadcast_to(jnp.eye(a * b, dtype=w.dtype).reshape(a, b, a, 1, b), (a, b, a, S5_GL, b))
    e = e.reshape(n, n * S5_GL)
    tn = 512
    return pl.pallas_call(
        functools.partial(_s5_widen_body, row_div=row_div, col_div=b),
        grid=(n8, n * S5_GL // tn),
        in_specs=[pl.BlockSpec((1, rows, n), lambda x, j: (x, 0, 0)), pl.BlockSpec((n, tn), lambda x, j: (0, j))],
        out_specs=pl.BlockSpec((1, rows, tn), lambda x, j: (x, 0, j)),
        out_shape=jax.ShapeDtypeStruct((n8, rows, n * S5_GL), w.dtype),
        compiler_params=_cparams(("parallel", "arbitrary")),
        name="s5_widen",
    )(w, e)


def _s5_gather_chunks(x_ref, xc_ref):
    rt = xc_ref.shape[0]
    parts = [x_ref[pl.ds(s, rt, stride=S5_CHUNK), :] for s in range(S5_CHUNK)]
    xc_ref[...] = jnp.concatenate(parts, axis=1).astype(xc_ref.dtype)


def _s5_state_in_body(u_ref, w_ref, vf_ref, vb_ref, xc_ref, pf_ref, pb_ref):
    _s5_gather_chunks(u_ref, xc_ref)
    rt = xc_ref.shape[0]
    v = _mm(xc_ref[...], w_ref[0])
    for n, (o_ref, p_ref) in enumerate(((vf_ref, pf_ref), (vb_ref, pb_ref))):
        for gi in range(S5_GL):
            c0 = (n * S5_GL + gi) * S5_ST2
            p_ref[pl.ds(gi, rt, stride=S5_GL), :] = v[:, c0:c0 + S5_ST2]
        o_ref[...] = p_ref[...].reshape(o_ref.shape)


def _s5_scan_body(vf_ref, vb_ref, afr_ref, afi_ref, abr_ref, abi_ref, sf_ref, sb_ref, cf_ref, cb_ref, *, rt):
    @pl.when(pl.program_id(1) == 0)
    def _():
        cf_ref[...] = jnp.zeros_like(cf_ref)
        cb_ref[...] = jnp.zeros_like(cb_ref)

    afr, afi, abr, abi = afr_ref[...], afi_ref[...], abr_ref[...], abi_ref[...]

    def step(k, carry):
        s_f, s_b = carry
        sf_ref[0, k] = s_f.astype(sf_ref.dtype)
        s_f = afr * s_f + afi * pltpu.roll(s_f, S5_STATE, 1) + vf_ref[0, k]
        kb = rt - 1 - k
        sb_ref[0, kb] = s_b.astype(sb_ref.dtype)
        s_b = abr * s_b + abi * pltpu.roll(s_b, S5_STATE, 1) + vb_ref[0, kb]
        return s_f, s_b

    s_f, s_b = lax.fori_loop(0, rt, step, (cf_ref[...], cb_ref[...]))
    cf_ref[...] = s_f
    cb_ref[...] = s_b


def _s5_out_body(u_ref, t_ref, sf_ref, sb_ref, wof_ref, wob_ref, d_ref, o_ref, xc_ref, pf_ref, pb_ref):
    _s5_gather_chunks(u_ref, xc_ref)
    rt = xc_ref.shape[0]
    mxd = xc_ref.dtype

    def by_group(s_ref, p_ref):
        p_ref[...] = s_ref[...].reshape(p_ref.shape)
        return jnp.concatenate([p_ref[pl.ds(gi, rt, stride=S5_GL), :] for gi in range(S5_GL)], axis=1).astype(mxd)

    s_f = by_group(sf_ref, pf_ref)
    s_b = by_group(sb_ref, pb_ref)
    xc = xc_ref[...]
    pw = S5_PANEL * 128
    for p0 in range(0, S5_CHUNK, S5_PANEL):
        cols = pl.ds(p0 * 128, pw)
        y = _mm(xc, t_ref[0, :, cols]) + _mm(s_f, wof_ref[0, :, cols]) + _mm(s_b, wob_ref[0, :, cols])
        for s in range(S5_PANEL):
            rows = pl.ds(p0 + s, rt, stride=S5_CHUNK)
            o_ref[rows, :] = jax.nn.gelu(y[:, s * 128:(s + 1) * 128] + d_ref[...] * u_ref[rows, :])


def _glu_body(y_ref, wv_ref, wg_ref, x_ref, o_ref, yb_ref):
    @pl.when(pl.program_id(1) == 0)
    def _():
        yb_ref[...] = y_ref[...].astype(yb_ref.dtype)

    y = yb_ref[...]
    val = _mm(y, wv_ref[...])
    gate = _mm(y, wg_ref[...])
    o_ref[...] = x_ref[...] + val * jax.nn.sigmoid(gate)


def _s5_mixer(x, bsz, length, g, w_in, tabs, w_glu):
    t, d = x.shape
    g_n, ck = S5_GROUPS, S5_CHUNK
    r_n = length // ck
    rows = r_n * bsz
    mx = _MXU_DTYPE
    n8 = g_n // S5_GL
    lanes = S5_GL * S5_GROUP
    k8 = ck * lanes
    st8 = S5_GL * S5_ST2
    u = _rms_mm(x, g, w_in, "s5_in", out_dtype=F32)

    rb = _pick(rows, 256)
    u_spec = pl.BlockSpec((rb * ck, lanes), lambda j, i: (i, j))
    st_spec = lambda: pl.BlockSpec((rb, S5_GL, S5_ST2), lambda j, i: (i, j, 0))
    held = lambda shape: pl.BlockSpec((1,) + shape, lambda j, i: (j, 0, 0))
    vf, vb = pl.pallas_call(
        _s5_state_in_body,
        grid=(n8, rows // rb),
        in_specs=[u_spec, held((k8, 2 * st8))],
        out_specs=[st_spec(), st_spec()],
        out_shape=[jax.ShapeDtypeStruct((rows, g_n, S5_ST2), F32)] * 2,
        scratch_shapes=[pltpu.VMEM((rb, k8), mx)] + [pltpu.VMEM((rb * S5_GL, S5_ST2), F32)] * 2,
        compiler_params=_cparams(("parallel", "arbitrary")),
        name="s5_state_in",
    )(u, tabs["w_state"])

    rt = _pick(r_n, 16)
    n_rt = r_n // rt
    v4 = (bsz, r_n, g_n, S5_ST2)
    blk = (1, rt, g_n, S5_ST2)
    fwd = lambda b, i: (b, i, 0, 0)
    bwd = lambda b, i: (b, n_rt - 1 - i, 0, 0)
    coef_spec = pl.BlockSpec((g_n, S5_ST2), lambda b, i: (0, 0))
    sf, sb = pl.pallas_call(
        functools.partial(_s5_scan_body, rt=rt),
        grid=(bsz, n_rt),
        in_specs=[pl.BlockSpec(blk, fwd), pl.BlockSpec(blk, bwd), coef_spec, coef_spec, coef_spec, coef_spec],
        out_specs=[pl.BlockSpec(blk, fwd), pl.BlockSpec(blk, bwd)],
        out_shape=[jax.ShapeDtypeStruct(v4, F32)] * 2,
        scratch_shapes=[pltpu.VMEM((g_n, S5_ST2), F32)] * 2,
        compiler_params=_cparams(("parallel", "arbitrary")),
        name="s5_scan",
    )(vf.reshape(v4), vb.reshape(v4), tabs["a_f"][0], tabs["a_f"][1], tabs["a_b"][0], tabs["a_b"][1])
    sf = sf.reshape(rows, g_n, S5_ST2)
    sb = sb.reshape(rows, g_n, S5_ST2)

    y = pl.pallas_call(
        _s5_out_body,
        grid=(n8, rows // rb),
        in_specs=[u_spec, held((k8, k8)), st_spec(), st_spec(), held((st8, k8)), held((st8, k8)),
                  pl.BlockSpec((1, lanes), lambda j, i: (0, j))],
        out_specs=u_spec,
        out_shape=jax.ShapeDtypeStruct((t, d), F32),
        scratch_shapes=[pltpu.VMEM((rb, k8), mx)] + [pltpu.VMEM((rb * S5_GL, S5_ST2), F32)] * 2,
        compiler_params=_cparams(("parallel", "arbitrary")),
        name="s5_out",
    )(u, tabs["toep"], sf, sb, tabs["wo_f"], tabs["wo_b"], tabs["d"])

    tm = _pick(t, 1024)
    tn = 512
    n_j = d // tn
    return pl.pallas_call(
        _glu_body,
        grid=(t // tm, n_j),
        in_specs=[
            pl.BlockSpec((tm, d), lambda i, j: (i, 0)),
            pl.BlockSpec((d, tn), lambda i, j: (0, j)),
            pl.BlockSpec((d, tn), lambda i, j: (0, j + n_j)),
            pl.BlockSpec((tm, tn), lambda i, j: (i, j)),
        ],
        out_specs=pl.BlockSpec((tm, tn), lambda i, j: (i, j)),
        out_shape=jax.ShapeDtypeStruct((t, d), F32),
        scratch_shapes=[pltpu.VMEM((tm, d), mx)],
        compiler_params=_cparams(("parallel", "arbitrary")),
        name="s5_glu",
    )(y, w_glu, w_glu, x)


def _log_sigmoid(x):
    return jnp.minimum(x, 0.0) - jnp.log(1.0 + jnp.exp(-jnp.abs(x)))


def _ml_pre_body(xm_ref, prev_ref, next_ref, cw_ref, cb_ref, bdq_ref, bdkt_ref, bdv_ref, wgt_ref, bgt_ref,
                 q_ref, kt_ref, v_ref, xc_ref, gatest_ref, pt_ref, st_ref, ext_ref, *, tiles_per_seq):
    tm = xm_ref.shape[0]
    mxd = xm_ref.dtype
    pos = lax.rem(pl.program_id(0), tiles_per_seq)
    ext_ref[pl.ds(HALO, tm), :] = xm_ref[...].astype(F32)
    ext_ref[pl.ds(0, HALO), :] = jnp.where(pos == 0, 0.0, prev_ref[...].astype(F32))
    ext_ref[pl.ds(HALO + tm, HALO), :] = jnp.where(pos == tiles_per_seq - 1, 0.0, next_ref[...].astype(F32))
    nt = (((1,), (1,)), ((), ()))
    scale = ML_DH ** -0.5

    def conv(h):
        sl = pl.ds(h * ML_DH, ML_DH)
        pre = cb_ref[:, sl]
        for jj in range(ML_CONV):
            pre = pre + cw_ref[jj:jj + 1, sl] * ext_ref[pl.ds(HALO - ML_CONV // 2 + jj, tm), sl]
        xc = (pre * jax.nn.sigmoid(pre)).astype(mxd)
        xc_ref[:, sl] = xc
        return xc

    def project(h, xc):
        sl = pl.ds(h * ML_DH, ML_DH)
        qh = _mm(xc, bdq_ref[h]).astype(mxd)
        vh = _mm(xm_ref[:, sl], bdv_ref[h]).astype(mxd)
        kt = (lax.dot_general(bdkt_ref[h], xc, nt, preferred_element_type=F32) * scale).astype(mxd)
        q_ref[:, sl] = qh
        v_ref[:, sl] = vh
        for c in range(tm // ML_CHUNK):
            kt_ref[c, sl, :] = kt[:, c * ML_CHUNK:(c + 1) * ML_CHUNK]
        return (lax.dot_general(wgt_ref[0, :, sl], qh, nt, preferred_element_type=F32)
                + _mm(wgt_ref[1, :, sl], kt) * (1.0 / scale)
                + lax.dot_general(wgt_ref[2, :, sl], vh, nt, preferred_element_type=F32))

    gates_t = jnp.broadcast_to(bgt_ref[...], (ML_NGATE, tm))
    xc_prev = conv(0)
    for h in range(1, ML_HEADS):
        xc_next = conv(h)
        gates_t = gates_t + project(h - 1, xc_prev)
        xc_prev = xc_next
    gates_t = gates_t + project(ML_HEADS - 1, xc_prev)

    r_i = lax.broadcasted_iota(jnp.int32, (ML_CHUNK, ML_CHUNK), 0)
    c_i = lax.broadcasted_iota(jnp.int32, (ML_CHUNK, ML_CHUNK), 1)
    incl_before = (r_i <= c_i).astype(F32)
    incl_after = (r_i >= c_i).astype(F32)
    gatest_ref[...] = gates_t
    for c in range(tm // ML_CHUNK):
        sl = pl.ds(c * ML_CHUNK, ML_CHUNK)
        lst = _log_sigmoid(gates_t[:, c * ML_CHUNK:(c + 1) * ML_CHUNK])
        pt_ref[:, sl] = _mm_exact(lst, incl_before)
        st_ref[:, sl] = _mm_exact(lst, incl_after)


def _mlstm_body(qf, kf, vf, rowf, colf, qb, kb, vb, rowb, colb, hf_ref, hb_ref, m_ref, *c_refs):
    @pl.when(pl.program_id(2) == 0)
    def _():
        m_ref[...] = jnp.zeros_like(m_ref)
        for c_ref in c_refs:
            c_ref[...] = jnp.zeros_like(c_ref)

    io = ((qf, kf, vf, rowf, hf_ref, colf), (qb, kb, vb, rowb, hb_ref, colb))
    chains = [(hd, d) for hd in range(ML_HP) for d in (0, 1)]
    t_i = lax.broadcasted_iota(jnp.int32, (ML_CHUNK, ML_CHUNK), 0)
    s_i = lax.broadcasted_iota(jnp.int32, (ML_CHUNK, ML_CHUNK), 1)
    masks = (s_i <= t_i, s_i >= t_i)
    m_all = m_ref[...]
    mxd = qf.dtype
    sq = (ML_CHUNK, ML_AUG)
    ones_blk = jnp.ones(sq, mxd)

    st = []
    for i, (hd, d) in enumerate(chains):
        row = io[d][3][hd]
        col = io[d][5][hd]
        bc = col[:, d:d + 1]
        lic = col[:, 2 + d:3 + d]
        bcr = row[d:d + 1, :]
        lir = row[2 + d:3 + d, :]
        g = bcr[:, ML_CHUNK - 1:ML_CHUNK] if d == 0 else bcr[:, 0:1]
        m_prev = m_all[i:i + 1, 0:1]
        bc_b = jnp.broadcast_to(bc, sq)
        a_b = bc_b + m_prev
        drow = lir - bcr
        dm = jnp.where(masks[d], bc_b + drow, -jnp.inf)
        m_t = jnp.maximum(a_b, jnp.max(dm, axis=1, keepdims=True))
        dec = jnp.exp(dm - m_t)
        e = jnp.exp(a_b - m_t)
        m_new = jnp.maximum(g + m_prev, jnp.max(g + drow, axis=1, keepdims=True))
        decay = jnp.exp(g + m_prev - m_new)
        wr = jnp.exp(g - bc_b + jnp.broadcast_to(lic, sq) - m_new)
        st.append(dict(m_t=m_t, dec=dec, e=e, m_new=m_new, decay=decay, wr=wr))

    for i, (hd, d) in enumerate(chains):
        sl = pl.ds(hd * ML_DH, ML_DH)
        s = _mm(io[d][0][:, sl], io[d][1][sl, :]) * st[i]["dec"]
        st[i]["s"] = s.astype(mxd)

    for i, (hd, d) in enumerate(chains):
        sl = pl.ds(hd * ML_DH, ML_DH)
        v_aug = jnp.concatenate([io[d][2][:, sl], ones_blk], axis=1)
        e3 = jnp.concatenate([st[i]["e"]] * (ML_DH // ML_AUG + 1), axis=1)
        nd = e3 * _mm(io[d][0][:, sl], c_refs[i][...].astype(mxd)) + _mm(st[i]["s"], v_aug)
        inv = 1.0 / jnp.maximum(jnp.abs(nd[:, ML_DH:]), jnp.exp(-st[i]["m_t"]))
        inv2 = jnp.concatenate([inv] * (ML_DH // ML_AUG), axis=1)
        io[d][4][:, sl] = (nd[:, :ML_DH] * inv2).astype(io[d][4].dtype)

    for i, (hd, d) in enumerate(chains):
        sl = pl.ds(hd * ML_DH, ML_DH)
        v_aug = jnp.concatenate([io[d][2][:, sl], ones_blk], axis=1)
        wr3 = jnp.concatenate([st[i]["wr"]] * (ML_DH // ML_AUG + 1), axis=1)
        vw = (v_aug.astype(F32) * wr3).astype(mxd)
        c_refs[i][...] = st[i]["decay"] * c_refs[i][...] + _mm(io[d][1][sl, :], vw)
    m_ref[...] = jnp.concatenate([jnp.broadcast_to(c["m_new"], (1, m_all.shape[1])) for c in st], axis=0)


def _ml_out_body(hf_ref, hb_ref, z_ref, xc_ref, ng_ref, sk_ref, w_ref, x_ref, o_ref):
    @pl.when(pl.program_id(1) == 0)
    def _():
        o_ref[...] = x_ref[...]

    acc = o_ref[...]
    for hd in range(ML_OG):
        sl = pl.ds(hd * ML_DH, ML_DH)
        hh = hf_ref[:, sl].astype(F32) + hb_ref[:, sl].astype(F32)
        mu = jnp.mean(hh, axis=-1, keepdims=True)
        cen = hh - mu
        var = jnp.mean(cen * cen, axis=-1, keepdims=True)
        hn = cen * lax.rsqrt(var + EPS) * ng_ref[:, sl]
        out = jax.nn.sigmoid(z_ref[:, sl].astype(F32)) * (hn + sk_ref[:, sl] * xc_ref[:, sl].astype(F32))
        acc = acc + _mm(out.astype(w_ref.dtype), w_ref[sl, :])
    o_ref[...] = acc


def _blockdiag(w):
    nb = ML_DH // ML_BLK
    wt = w.astype(F32).reshape(ML_HEADS, nb, ML_BLK, ML_BLK)
    eye = jnp.eye(nb, dtype=F32)
    return jnp.einsum("hncd,nm->hncmd", wt, eye).reshape(ML_HEADS, ML_DH, ML_DH).astype(_MXU_DTYPE)


def _mlstm_mixer(x, bsz, length, g, w_in, conv_w, conv_b, wq, wk, wv, w_gates, b_gates, norm_g, skip, w_out):
    t, d = x.shape
    mx = _MXU_DTYPE
    di, dh, nh = ML_INNER, ML_DH, ML_HEADS
    xz = _rms_mm(x, g, w_in, "ml_in")

    tm = _pick(length, 256)
    tiles_per_seq = length // tm
    hb = tm // HALO
    n_h = t // HALO
    wgt = w_gates.astype(mx).transpose(0, 2, 1)
    row_spec = lambda: pl.BlockSpec((tm, di), lambda i: (i, 0))
    small_t = lambda: pl.BlockSpec((ML_NGATE, tm), lambda i: (0, i))
    whole = lambda a: pl.BlockSpec(a.shape, lambda i: (0,) * a.ndim)
    consts = (conv_w.astype(F32), conv_b.astype(F32).reshape(1, di), _blockdiag(wq),
              _blockdiag(wk).transpose(0, 2, 1), _blockdiag(wv), wgt, b_gates.astype(F32).reshape(ML_NGATE, 1))
    outs = pl.pallas_call(
        functools.partial(_ml_pre_body, tiles_per_seq=tiles_per_seq),
        grid=(t // tm,),
        in_specs=[
            row_spec(),
            pl.BlockSpec((HALO, di), lambda i: (jnp.maximum(i * hb - 1, 0), 0)),
            pl.BlockSpec((HALO, di), lambda i: (jnp.minimum((i + 1) * hb, n_h - 1), 0)),
        ] + [whole(a) for a in consts],
        out_specs=[row_spec(), pl.BlockSpec((tm // ML_CHUNK, di, ML_CHUNK), lambda i: (i, 0, 0)), row_spec(),
                   row_spec(), small_t(), small_t(), small_t()],
        out_shape=[jax.ShapeDtypeStruct((t, di), mx), jax.ShapeDtypeStruct((t // ML_CHUNK, di, ML_CHUNK), mx)]
        + [jax.ShapeDtypeStruct((t, di), mx)] * 2 + [jax.ShapeDtypeStruct((ML_NGATE, t), F32)] * 3,
        scratch_shapes=[pltpu.VMEM((tm + 2 * HALO, di), F32)],
        compiler_params=_cparams(("parallel",)),
        name="ml_pre",
    )(xz, xz, xz, *consts)
    q, kt, v, xc, gates_t, pre_t, suf_t = outs
    rows = jnp.stack([pre_t[nh:2 * nh], suf_t[3 * nh:], gates_t[:nh], gates_t[2 * nh:3 * nh]], axis=1)
    cols = rows.transpose(0, 2, 1)

    nc = length // ML_CHUNK
    hp = ML_HP
    fw = lambda b, h, j: (b * nc + j, h)
    bw = lambda b, h, j: (b * nc + nc - 1 - j, h)
    qkv = lambda f: pl.BlockSpec((ML_CHUNK, hp * dh), f)
    ktspec = lambda f: pl.BlockSpec((None, hp * dh, ML_CHUNK), lambda b, h, j: f(b, h, j) + (0,))
    rowspec = lambda f: pl.BlockSpec((hp, 4, ML_CHUNK), lambda b, h, j: (h, 0, f(b, h, j)[0]))
    colspec = lambda f: pl.BlockSpec((hp, ML_CHUNK, 4), lambda b, h, j: (h, f(b, h, j)[0], 0))
    hf, hbk = pl.pallas_call(
        _mlstm_body,
        grid=(bsz, nh // hp, nc),
        in_specs=[qkv(fw), ktspec(fw), qkv(fw), rowspec(fw), colspec(fw),
                  qkv(bw), ktspec(bw), qkv(bw), rowspec(bw), colspec(bw)],
        out_specs=[qkv(fw), qkv(bw)],
        out_shape=[jax.ShapeDtypeStruct((t, di), mx)] * 2,
        scratch_shapes=[pltpu.VMEM((2 * hp, 128), F32)] + [pltpu.VMEM((dh, dh + ML_AUG), F32)] * (2 * hp),
        compiler_params=_cparams(("parallel", "parallel", "arbitrary")),
        name="mlstm",
    )(q, kt, v, rows, cols, q, kt, v, rows, cols)

    tmo = _pick(t, 512)
    kg = ML_OG * dh
    n_g = nh // ML_OG
    grp = lambda: pl.BlockSpec((tmo, kg), lambda i, j: (i, j))
    return pl.pallas_call(
        _ml_out_body,
        grid=(t // tmo, n_g),
        in_specs=[
            grp(), grp(),
            pl.BlockSpec((tmo, kg), lambda i, j: (i, j + n_g)),
            grp(),
            pl.BlockSpec((1, kg), lambda i, j: (0, j)),
            pl.BlockSpec((1, kg), lambda i, j: (0, j)),
            pl.BlockSpec((kg, d), lambda i, j: (j, 0)),
            pl.BlockSpec((tmo, d), lambda i, j: (i, 0)),
        ],
        out_specs=pl.BlockSpec((tmo, d), lambda i, j: (i, 0)),
        out_shape=jax.ShapeDtypeStruct((t, d), F32),
        compiler_params=_cparams(("parallel", "arbitrary")),
        name="ml_out",
    )(hf, hbk, xz, xc, norm_g.astype(F32).reshape(1, di), skip.astype(F32).reshape(1, di), w_out, x)


def _trunk(x3, p):
    bsz, length, d = x3.shape
    x = x3.reshape(bsz * length, d)
    ffn = lambda x, layer, idx, **kw: _ffn(x, p["norm_g"][layer, 2 * idx], p["ffn_w_in"], p["ffn_w_out"],
                                           2 * layer + idx, **kw)
    x = ffn(x, 0, 0)
    x = _s5_mixer(x, bsz, length, p["norm_g"][0, 1], p["s5_w_in"], p["s5_tabs"], p["s5_w_glu"])
    x = ffn(x, 0, 1)
    x = ffn(x, 1, 0)
    x = _mlstm_mixer(x, bsz, length, p["norm_g"][1, 1], p["ml_w_in"], *p["ml_rest"])
    x = ffn(x, 1, 1, final_g=p["final_g"])
    return x.reshape(bsz, length, d)


def kernel(x_prompt, x_sample, norm_g, final_g, ffn_w_in, ffn_w_out, s5_w_in, s5_lambda_re, s5_lambda_im, s5_log_step, s5_b_re, s5_b_im, s5_c_re, s5_c_im, s5_d, s5_w_glu, ml_w_in, ml_conv_w, ml_conv_b, ml_wq, ml_wk, ml_wv, ml_w_gates, ml_b_gates, ml_norm_g, ml_skip, ml_w_out):
    mx = _MXU_DTYPE
    p = dict(
        norm_g=norm_g.astype(F32), final_g=final_g.astype(F32),
        ffn_w_in=ffn_w_in.astype(mx).reshape(-1, D_MODEL, 2 * D_FF),
        ffn_w_out=ffn_w_out.astype(mx).reshape(-1, D_FF, D_MODEL),
        s5_w_in=s5_w_in[0].astype(mx), s5_w_glu=s5_w_glu[0].astype(mx),
        s5_tabs=_s5_tables(s5_lambda_re[0], s5_lambda_im[0], s5_log_step[0], s5_b_re[0], s5_b_im[0], s5_c_re[0],
                           s5_c_im[0], s5_d[0]),
        ml_w_in=ml_w_in[0].astype(mx),
        ml_rest=(ml_conv_w[0], ml_conv_b[0], ml_wq[0], ml_wk[0], ml_wv[0], ml_w_gates[0], ml_b_gates[0],
                 ml_norm_g[0], ml_skip[0], ml_w_out[0].astype(mx)),
    )
    return (_trunk(x_prompt, p), _trunk(x_sample, p))
```

```python
import functools

import jax
import jax.numpy as jnp
from jax import lax
from jax.experimental import pallas as pl
from jax.experimental.pallas import tpu as pltpu

F32 = jnp.float32
_MXU_DTYPE = jnp.bfloat16

D_MODEL = 2048
D_FF = 5632
EPS = 1e-6
S5_GROUP = 16
S5_GROUPS = D_MODEL // S5_GROUP
S5_STATE = 64
S5_CHUNK = 16
S5_ROW = S5_CHUNK * S5_GROUP
S5_ST2 = 2 * S5_STATE
S5_GL = 8
S5_PANEL = 4
ML_INNER = 2 * D_MODEL
ML_HEADS = 16
ML_DH = ML_INNER // ML_HEADS
ML_BLK = 4
ML_CONV = 5
ML_CHUNK = 128
ML_NGATE = 4 * ML_HEADS
ML_HP = 4
ML_OG = 4
ML_AUG = 128
HALO = 16

_VMEM_LIMIT = 52 * 1024 * 1024
_VMEM_LIMIT_FFN = 60 * 1024 * 1024


def _cparams(sem, limit=_VMEM_LIMIT):
    return pltpu.CompilerParams(dimension_semantics=sem, vmem_limit_bytes=limit)


def _mm(a, b):
    return jnp.dot(a, b, preferred_element_type=F32)


def _mm_exact(a, b):
    return jnp.dot(a, b, preferred_element_type=F32, precision=lax.Precision.HIGHEST)


def _rms(x, g):
    ms = jnp.mean(x * x, axis=-1, keepdims=True)
    return x * lax.rsqrt(ms + EPS) * g


def _sigmoid(x):
    return 0.5 * jnp.tanh(0.5 * x) + 0.5


def _pick(n, pref):
    t = min(n, pref)
    while n % t:
        t //= 2
    return t


def _ffn_body(x_ref, g_ref, wg_ref, wu_ref, wo_ref, fg_ref, o_ref, hn_ref, *, n_j, final_norm):
    j = pl.program_id(1)

    @pl.when(j == 0)
    def _():
        hn_ref[...] = _rms(x_ref[...], g_ref[...]).astype(hn_ref.dtype)
        o_ref[...] = jnp.zeros_like(o_ref)

    h = hn_ref[...]
    gate = _mm(h, wg_ref[...])
    up = _mm(h, wu_ref[...])
    act = (gate * _sigmoid(gate) * up).astype(h.dtype)
    o_ref[...] += _mm(act, wo_ref[...])

    @pl.when(j == n_j - 1)
    def _():
        y = x_ref[...] + 0.5 * o_ref[...]
        if final_norm:
            y = _rms(y, fg_ref[...])
        o_ref[...] = y


def _ffn(x, g, w_in, w_out, k, final_g=None):
    t, d = x.shape
    tm = _pick(t, 1024 if final_g is None else 512)
    tn = 512
    n_j = D_FF // tn
    fg = g if final_g is None else final_g
    body = functools.partial(_ffn_body, n_j=n_j, final_norm=final_g is not None)
    return pl.pallas_call(
        body,
        grid=(t // tm, n_j),
        in_specs=[
            pl.BlockSpec((tm, d), lambda i, j: (i, 0)),
            pl.BlockSpec((1, d), lambda i, j: (0, 0)),
            pl.BlockSpec((None, d, tn), lambda i, j: (k, 0, j)),
            pl.BlockSpec((None, d, tn), lambda i, j: (k, 0, j + n_j)),
            pl.BlockSpec((None, tn, d), lambda i, j: (k, j, 0)),
            pl.BlockSpec((1, d), lambda i, j: (0, 0)),
        ],
        out_specs=pl.BlockSpec((tm, d), lambda i, j: (i, 0)),
        out_shape=jax.ShapeDtypeStruct((t, d), F32),
        scratch_shapes=[pltpu.VMEM((tm, d), _MXU_DTYPE)],
        compiler_params=_cparams(("parallel", "arbitrary"), _VMEM_LIMIT_FFN),
        name="ffn",
    )(x, g.reshape(1, d), w_in, w_in, w_out, fg.reshape(1, d))


def _rms_mm_body(x_ref, g_ref, w_ref, o_ref, hn_ref):
    @pl.when(pl.program_id(1) == 0)
    def _():
        hn_ref[...] = _rms(x_ref[...], g_ref[...]).astype(hn_ref.dtype)

    o_ref[...] = _mm(hn_ref[...], w_ref[...]).astype(o_ref.dtype)


def _rms_mm(x, g, w, name, out_dtype=None):
    t, d = x.shape
    n = w.shape[1]
    tm = _pick(t, 1024)
    tn = 1024
    return pl.pallas_call(
        _rms_mm_body,
        grid=(t // tm, n // tn),
        in_specs=[
            pl.BlockSpec((tm, d), lambda i, j: (i, 0)),
            pl.BlockSpec((1, d), lambda i, j: (0, 0)),
            pl.BlockSpec((d, tn), lambda i, j: (0, j)),
        ],
        out_specs=pl.BlockSpec((tm, tn), lambda i, j: (i, j)),
        out_shape=jax.ShapeDtypeStruct((t, n), out_dtype or _MXU_DTYPE),
        scratch_shapes=[pltpu.VMEM((tm, d), _MXU_DTYPE)],
        compiler_params=_cparams(("parallel", "arbitrary")),
        name=name,
    )(x, g.reshape(1, d), w)


def _s5_kern_body(cq_ref, pd_ref, o_ref):
    for n in range(cq_ref.shape[0]):
        o_ref[n] = _mm_exact(cq_ref[n], pd_ref[n])


def _s5_tables(lam_re, lam_im, log_step, b_re, b_im, c_re, c_im, d):
    g_n, p_n, gs, ck = S5_GROUPS, S5_STATE, S5_GROUP, S5_CHUNK
    lam = lax.complex(jnp.minimum(lam_re.astype(F32), -1e-4), lam_im.astype(F32))
    delta = jnp.exp(log_step.astype(F32))[..., None]
    lam_bar = jnp.exp(lam * delta)
    b_bar = ((lam_bar - 1.0) / lam)[..., None] * lax.complex(b_re.astype(F32), b_im.astype(F32))
    c = lax.complex(c_re.astype(F32), c_im.astype(F32))
    pows = [jnp.ones_like(lam_bar)]
    for _ in range(ck):
        pows.append(pows[-1] * lam_bar)
    pw = jnp.stack(pows)
    pw_rev = jnp.stack(pows[::-1])

    x = pw[:ck, :, :, :, None] * b_bar[None]
    xt = x.transpose(1, 2, 3, 0, 4).reshape(2, g_n, p_n, ck * gs)
    pd_t = jnp.concatenate([jnp.real(xt), jnp.imag(xt)], axis=2).reshape(2 * g_n, 2 * p_n, ck * gs)
    cq_t = jnp.concatenate([jnp.real(c), -jnp.imag(c)], axis=-1).reshape(2 * g_n, gs, 2 * p_n)
    gb = 8
    kt = pl.pallas_call(
        _s5_kern_body,
        grid=(2 * g_n // gb,),
        in_specs=[
            pl.BlockSpec((gb, gs, 2 * p_n), lambda i: (i, 0, 0)),
            pl.BlockSpec((gb, 2 * p_n, ck * gs), lambda i: (i, 0, 0)),
        ],
        out_specs=pl.BlockSpec((gb, gs, ck * gs), lambda i: (i, 0, 0)),
        out_shape=jax.ShapeDtypeStruct((2 * g_n, gs, ck * gs), F32),
        compiler_params=_cparams(("parallel",)),
        name="s5_kernels",
    )(cq_t, pd_t)
    kt = kt.reshape(2, g_n, gs, ck, gs)
    s_i = jnp.arange(ck)[:, None]
    t_i = jnp.arange(ck)[None, :]
    dfw = t_i - s_i
    dbw = s_i - t_i
    tf = jnp.where((dfw >= 0)[None, None, :, :, None], kt[0][:, :, jnp.clip(dfw, 0, ck - 1), :], 0.0)
    tb = jnp.where((dbw >= 0)[None, None, :, :, None], kt[1][:, :, jnp.clip(dbw, 0, ck - 1), :], 0.0)
    toep = (tf + tb).transpose(0, 2, 4, 3, 1).reshape(g_n, ck * gs, ck * gs)

    def reim_last(z):
        return jnp.concatenate([jnp.real(z), jnp.imag(z)], axis=-1)

    xf = pw_rev[1:ck + 1, 0, :, :, None] * b_bar[0][None]
    xf = xf.transpose(1, 0, 3, 2).reshape(g_n, ck * gs, p_n)
    xb = x[:, 1].transpose(1, 0, 3, 2).reshape(g_n, ck * gs, p_n)
    w_state = jnp.concatenate([reim_last(xf), reim_last(xb)], axis=-1)

    zf = (c[0][None] * pw[1:ck + 1, 0][:, :, None, :]).transpose(1, 3, 0, 2).reshape(g_n, p_n, ck * gs)
    zb = (c[1][None] * pw_rev[:ck, 1][:, :, None, :]).transpose(1, 3, 0, 2).reshape(g_n, p_n, ck * gs)
    wo_f = jnp.concatenate([jnp.real(zf), -jnp.imag(zf)], axis=1)
    wo_b = jnp.concatenate([jnp.real(zb), -jnp.imag(zb)], axis=1)

    def coef(a):
        return (jnp.concatenate([jnp.real(a), jnp.real(a)], -1), jnp.concatenate([-jnp.imag(a), jnp.imag(a)], -1))

    a_f = coef(pw[ck, 0])
    a_b = coef(pw[ck, 1])
    mx = _MXU_DTYPE
    n8, gl = g_n // S5_GL, S5_GL
    kk = ck * gl * gs

    def rows_sgc(w):
        n = w.shape[-1]
        return w.astype(mx).reshape(n8, gl, ck, gs, n).transpose(0, 2, 1, 3, 4).reshape(n8, kk, n)

    def rows_gp(w):
        return w.astype(mx).reshape(n8, gl * w.shape[1], w.shape[2])

    td = (ck, gs)
    ep = (2, 2 * p_n)
    return dict(toep=_s5_widen(rows_sgc(toep), gs, td), w_state=_s5_widen(rows_sgc(w_state), gs, ep),
                wo_f=_s5_widen(rows_gp(wo_f), 2 * p_n, td), wo_b=_s5_widen(rows_gp(wo_b), 2 * p_n, td),
                a_f=a_f, a_b=a_b, d=d.astype(F32).reshape(1, g_n * gs))


def _s5_widen_body(w_ref, e_ref, o_ref, *, row_div, col_div):
    y = _mm(w_ref[0], e_ref[...])
    r_g = (lax.broadcasted_iota(jnp.int32, y.shape, 0) // row_div) % S5_GL
    c_g = ((lax.broadcasted_iota(jnp.int32, y.shape, 1) + pl.program_id(1) * y.shape[1]) // col_div) % S5_GL
    o_ref[0] = jnp.where(r_g == c_g, y, 0.0).astype(o_ref.dtype)


def _s5_widen(w, row_div, col_dims):
    n8, rows, n = w.shape
    a, b = col_dims
    e = jnp.broadcast_to(jnp.eye(a * b, dtype=w.dtype).reshape(a, b, a, 1, b), (a, b, a, S5_GL, b))
    e = e.reshape(n, n * S5_GL)
    tn = 512
    return pl.pallas_call(
        functools.partial(_s5_widen_body, row_div=row_div, col_div=b),
        grid=(n8, n * S5_GL // tn),
        in_specs=[pl.BlockSpec((1, rows, n), lambda x, j: (x, 0, 0)), pl.BlockSpec((n, tn), lambda x, j: (0, j))],
        out_specs=pl.BlockSpec((1, rows, tn), lambda x, j: (x, 0, j)),
        out_shape=jax.ShapeDtypeStruct((n8, rows, n * S5_GL), w.dtype),
        compiler_params=_cparams(("parallel", "arbitrary")),
        name="s5_widen",
    )(w, e)


def _s5_gather_chunks(x_ref, xc_ref):
    rt = xc_ref.shape[0]
    parts = [x_ref[pl.ds(s, rt, stride=S5_CHUNK), :] for s in range(S5_CHUNK)]
    xc_ref[...] = jnp.concatenate(parts, axis=1).astype(xc_ref.dtype)


def _s5_state_in_body(u_ref, w_ref, vf_ref, vb_ref, xc_ref, pf_ref, pb_ref):
    _s5_gather_chunks(u_ref, xc_ref)
    rt = xc_ref.shape[0]
    v = _mm(xc_ref[...], w_ref[0])
    for n, (o_ref, p_ref) in enumerate(((vf_ref, pf_ref), (vb_ref, pb_ref))):
        for gi in range(S5_GL):
            c0 = (n * S5_GL + gi) * S5_ST2
            p_ref[pl.ds(gi, rt, stride=S5_GL), :] = v[:, c0:c0 + S5_ST2]
        o_ref[...] = p_ref[...].reshape(o_ref.shape)


def _s5_scan_body(vf_ref, vb_ref, afr_ref, afi_ref, abr_ref, abi_ref, sf_ref, sb_ref, cf_ref, cb_ref, *, rt):
    @pl.when(pl.program_id(1) == 0)
    def _():
        cf_ref[...] = jnp.zeros_like(cf_ref)
        cb_ref[...] = jnp.zeros_like(cb_ref)

    afr, afi, abr, abi = afr_ref[...], afi_ref[...], abr_ref[...], abi_ref[...]

    def step(k, carry):
        s_f, s_b = carry
        sf_ref[0, k] = s_f.astype(sf_ref.dtype)
        s_f = afr * s_f + afi * pltpu.roll(s_f, S5_STATE, 1) + vf_ref[0, k]
        kb = rt - 1 - k
        sb_ref[0, kb] = s_b.astype(sb_ref.dtype)
        s_b = abr * s_b + abi * pltpu.roll(s_b, S5_STATE, 1) + vb_ref[0, kb]
        return s_f, s_b

    s_f, s_b = lax.fori_loop(0, rt, step, (cf_ref[...], cb_ref[...]))
    cf_ref[...] = s_f
    cb_ref[...] = s_b


def _s5_out_body(u_ref, t_ref, sf_ref, sb_ref, wof_ref, wob_ref, d_ref, o_ref, xc_ref, pf_ref, pb_ref):
    _s5_gather_chunks(u_ref, xc_ref)
    rt = xc_ref.shape[0]
    mxd = xc_ref.dtype

    def by_group(s_ref, p_ref):
        p_ref[...] = s_ref[...].reshape(p_ref.shape)
        return jnp.concatenate([p_ref[pl.ds(gi, rt, stride=S5_GL), :] for gi in range(S5_GL)], axis=1).astype(mxd)

    s_f = by_group(sf_ref, pf_ref)
    s_b = by_group(sb_ref, pb_ref)
    xc = xc_ref[...]
    pw = S5_PANEL * 128
    for p0 in range(0, S5_CHUNK, S5_PANEL):
        cols = pl.ds(p0 * 128, pw)
        y = _mm(xc, t_ref[0, :, cols]) + _mm(s_f, wof_ref[0, :, cols]) + _mm(s_b, wob_ref[0, :, cols])
        for s in range(S5_PANEL):
            rows = pl.ds(p0 + s, rt, stride=S5_CHUNK)
            o_ref[rows, :] = jax.nn.gelu(y[:, s * 128:(s + 1) * 128] + d_ref[...] * u_ref[rows, :])


def _glu_body(y_ref, wv_ref, wg_ref, x_ref, o_ref, yb_ref):
    @pl.when(pl.program_id(1) == 0)
    def _():
        yb_ref[...] = y_ref[...].astype(yb_ref.dtype)

    y = yb_ref[...]
    val = _mm(y, wv_ref[...])
    gate = _mm(y, wg_ref[...])
    o_ref[...] = x_ref[...] + val * _sigmoid(gate)


def _s5_mixer(x, bsz, length, g, w_in, tabs, w_glu):
    t, d = x.shape
    g_n, ck = S5_GROUPS, S5_CHUNK
    r_n = length // ck
    rows = r_n * bsz
    mx = _MXU_DTYPE
    n8 = g_n // S5_GL
    lanes = S5_GL * S5_GROUP
    k8 = ck * lanes
    st8 = S5_GL * S5_ST2
    u = _rms_mm(x, g, w_in, "s5_in", out_dtype=F32)

    rb = _pick(rows, 256)
    u_spec = pl.BlockSpec((rb * ck, lanes), lambda j, i: (i, j))
    st_spec = lambda: pl.BlockSpec((rb, S5_GL, S5_ST2), lambda j, i: (i, j, 0))
    held = lambda shape: pl.BlockSpec((1,) + shape, lambda j, i: (j, 0, 0))
    vf, vb = pl.pallas_call(
        _s5_state_in_body,
        grid=(n8, rows // rb),
        in_specs=[u_spec, held((k8, 2 * st8))],
        out_specs=[st_spec(), st_spec()],
        out_shape=[jax.ShapeDtypeStruct((rows, g_n, S5_ST2), F32)] * 2,
        scratch_shapes=[pltpu.VMEM((rb, k8), mx)] + [pltpu.VMEM((rb * S5_GL, S5_ST2), F32)] * 2,
        compiler_params=_cparams(("parallel", "arbitrary")),
        name="s5_state_in",
    )(u, tabs["w_state"])

    rt = _pick(r_n, 16)
    n_rt = r_n // rt
    v4 = (bsz, r_n, g_n, S5_ST2)
    blk = (1, rt, g_n, S5_ST2)
    fwd = lambda b, i: (b, i, 0, 0)
    bwd = lambda b, i: (b, n_rt - 1 - i, 0, 0)
    coef_spec = pl.BlockSpec((g_n, S5_ST2), lambda b, i: (0, 0))
    sf, sb = pl.pallas_call(
        functools.partial(_s5_scan_body, rt=rt),
        grid=(bsz, n_rt),
        in_specs=[pl.BlockSpec(blk, fwd), pl.BlockSpec(blk, bwd), coef_spec, coef_spec, coef_spec, coef_spec],
        out_specs=[pl.BlockSpec(blk, fwd), pl.BlockSpec(blk, bwd)],
        out_shape=[jax.ShapeDtypeStruct(v4, F32)] * 2,
        scratch_shapes=[pltpu.VMEM((g_n, S5_ST2), F32)] * 2,
        compiler_params=_cparams(("parallel", "arbitrary")),
        name="s5_scan",
    )(vf.reshape(v4), vb.reshape(v4), tabs["a_f"][0], tabs["a_f"][1], tabs["a_b"][0], tabs["a_b"][1])
    sf = sf.reshape(rows, g_n, S5_ST2)
    sb = sb.reshape(rows, g_n, S5_ST2)

    y = pl.pallas_call(
        _s5_out_body,
        grid=(n8, rows // rb),
        in_specs=[u_spec, held((k8, k8)), st_spec(), st_spec(), held((st8, k8)), held((st8, k8)),
                  pl.BlockSpec((1, lanes), lambda j, i: (0, j))],
        out_specs=u_spec,
        out_shape=jax.ShapeDtypeStruct((t, d), F32),
        scratch_shapes=[pltpu.VMEM((rb, k8), mx)] + [pltpu.VMEM((rb * S5_GL, S5_ST2), F32)] * 2,
        compiler_params=_cparams(("parallel", "arbitrary")),
        name="s5_out",
    )(u, tabs["toep"], sf, sb, tabs["wo_f"], tabs["wo_b"], tabs["d"])

    tm = _pick(t, 1024)
    tn = 512
    n_j = d // tn
    return pl.pallas_call(
        _glu_body,
        grid=(t // tm, n_j),
        in_specs=[
            pl.BlockSpec((tm, d), lambda i, j: (i, 0)),
            pl.BlockSpec((d, tn), lambda i, j: (0, j)),
            pl.BlockSpec((d, tn), lambda i, j: (0, j + n_j)),
            pl.BlockSpec((tm, tn), lambda i, j: (i, j)),
        ],
        out_specs=pl.BlockSpec((tm, tn), lambda i, j: (i, j)),
        out_shape=jax.ShapeDtypeStruct((t, d), F32),
        scratch_shapes=[pltpu.VMEM((tm, d), mx)],
        compiler_params=_cparams(("parallel", "arbitrary")),
        name="s5_glu",
    )(y, w_glu, w_glu, x)


def _log_sigmoid(x):
    return jnp.minimum(x, 0.0) - jnp.log(1.0 + jnp.exp(-jnp.abs(x)))


def _ml_pre_body(xm_ref, prev_ref, next_ref, cw_ref, cb_ref, bdq_ref, bdkt_ref, bdv_ref, wgt_ref, bgt_ref,
                 q_ref, kt_ref, v_ref, xc_ref, gatest_ref, pt_ref, st_ref, ext_ref, *, tiles_per_seq):
    tm = xm_ref.shape[0]
    mxd = xm_ref.dtype
    pos = lax.rem(pl.program_id(0), tiles_per_seq)
    ext_ref[pl.ds(HALO, tm), :] = xm_ref[...].astype(F32)
    ext_ref[pl.ds(0, HALO), :] = jnp.where(pos == 0, 0.0, prev_ref[...].astype(F32))
    ext_ref[pl.ds(HALO + tm, HALO), :] = jnp.where(pos == tiles_per_seq - 1, 0.0, next_ref[...].astype(F32))
    nt = (((1,), (1,)), ((), ()))
    scale = ML_DH ** -0.5

    def conv(h):
        sl = pl.ds(h * ML_DH, ML_DH)
        pre = cb_ref[:, sl]
        for jj in range(ML_CONV):
            pre = pre + cw_ref[jj:jj + 1, sl] * ext_ref[pl.ds(HALO - ML_CONV // 2 + jj, tm), sl]
        xc = (pre * _sigmoid(pre)).astype(mxd)
        xc_ref[:, sl] = xc
        return xc

    def project(h, xc):
        sl = pl.ds(h * ML_DH, ML_DH)
        qh = _mm(xc, bdq_ref[h]).astype(mxd)
        vh = _mm(xm_ref[:, sl], bdv_ref[h]).astype(mxd)
        kt = (lax.dot_general(bdkt_ref[h], xc, nt, preferred_element_type=F32) * scale).astype(mxd)
        q_ref[:, sl] = qh
        v_ref[:, sl] = vh
        for c in range(tm // ML_CHUNK):
            kt_ref[c, sl, :] = kt[:, c * ML_CHUNK:(c + 1) * ML_CHUNK]
        return (lax.dot_general(wgt_ref[0, :, sl], qh, nt, preferred_element_type=F32)
                + _mm(wgt_ref[1, :, sl], kt) * (1.0 / scale)
                + lax.dot_general(wgt_ref[2, :, sl], vh, nt, preferred_element_type=F32))

    gates_t = jnp.broadcast_to(bgt_ref[...], (ML_NGATE, tm))
    xc_prev = conv(0)
    for h in range(1, ML_HEADS):
        xc_next = conv(h)
        gates_t = gates_t + project(h - 1, xc_prev)
        xc_prev = xc_next
    gates_t = gates_t + project(ML_HEADS - 1, xc_prev)

    r_i = lax.broadcasted_iota(jnp.int32, (ML_CHUNK, ML_CHUNK), 0)
    c_i = lax.broadcasted_iota(jnp.int32, (ML_CHUNK, ML_CHUNK), 1)
    incl_before = (r_i <= c_i).astype(F32)
    incl_after = (r_i >= c_i).astype(F32)
    gatest_ref[...] = gates_t
    for c in range(tm // ML_CHUNK):
        sl = pl.ds(c * ML_CHUNK, ML_CHUNK)
        lst = _log_sigmoid(gates_t[:, c * ML_CHUNK:(c + 1) * ML_CHUNK])
        pt_ref[:, sl] = _mm_exact(lst, incl_before)
        st_ref[:, sl] = _mm_exact(lst, incl_after)


def _mlstm_body(qf, kf, vf, rowf, colf, qb, kb, vb, rowb, colb, hf_ref, hb_ref, m_ref, *c_refs):
    @pl.when(pl.program_id(2) == 0)
    def _():
        m_ref[...] = jnp.zeros_like(m_ref)
        for c_ref in c_refs:
            c_ref[...] = jnp.zeros_like(c_ref)

    io = ((qf, kf, vf, rowf, hf_ref, colf), (qb, kb, vb, rowb, hb_ref, colb))
    chains = [(hd, d) for hd in range(ML_HP) for d in (0, 1)]
    t_i = lax.broadcasted_iota(jnp.int32, (ML_CHUNK, ML_CHUNK), 0)
    s_i = lax.broadcasted_iota(jnp.int32, (ML_CHUNK, ML_CHUNK), 1)
    masks = (s_i <= t_i, s_i >= t_i)
    m_all = m_ref[...]
    mxd = qf.dtype
    sq = (ML_CHUNK, ML_AUG)
    ones_blk = jnp.ones(sq, mxd)

    st = []
    for i, (hd, d) in enumerate(chains):
        row = io[d][3][hd]
        col = io[d][5][:, hd * 4:(hd + 1) * 4]
        bc = col[:, d:d + 1]
        lic = col[:, 2 + d:3 + d]
        bcr = row[d:d + 1, :]
        lir = row[2 + d:3 + d, :]
        g = bcr[:, ML_CHUNK - 1:ML_CHUNK] if d == 0 else bcr[:, 0:1]
        m_prev = m_all[i:i + 1, 0:1]
        bc_b = jnp.broadcast_to(bc, sq)
        a_b = bc_b + m_prev
        drow = lir - bcr
        dm = jnp.where(masks[d], bc_b + drow, -jnp.inf)
        m_t = jnp.maximum(a_b, jnp.max(dm, axis=1, keepdims=True))
        dec = jnp.exp(dm - m_t)
        e = jnp.exp(a_b - m_t)
        m_new = jnp.maximum(g + m_prev, jnp.max(g + drow, axis=1, keepdims=True))
        decay = jnp.exp(g + m_prev - m_new)
        wr = jnp.exp(g - bc_b + jnp.broadcast_to(lic, sq) - m_new)
        st.append(dict(m_t=m_t, dec=dec, e=e, m_new=m_new, decay=decay, wr=wr))

    for i, (hd, d) in enumerate(chains):
        sl = pl.ds(hd * ML_DH, ML_DH)
        s = _mm(io[d][0][:, sl], io[d][1][sl, :]) * st[i]["dec"]
        st[i]["s"] = s.astype(mxd)

    for i, (hd, d) in enumerate(chains):
        sl = pl.ds(hd * ML_DH, ML_DH)
        v_aug = jnp.concatenate([io[d][2][:, sl], ones_blk], axis=1)
        e3 = jnp.concatenate([st[i]["e"]] * (ML_DH // ML_AUG + 1), axis=1)
        nd = e3 * _mm(io[d][0][:, sl], c_refs[i][...].astype(mxd)) + _mm(st[i]["s"], v_aug)
        inv = 1.0 / jnp.maximum(jnp.abs(nd[:, ML_DH:]), jnp.exp(-st[i]["m_t"]))
        inv2 = jnp.concatenate([inv] * (ML_DH // ML_AUG), axis=1)
        io[d][4][:, sl] = (nd[:, :ML_DH] * inv2).astype(io[d][4].dtype)

    for i, (hd, d) in enumerate(chains):
        sl = pl.ds(hd * ML_DH, ML_DH)
        v_aug = jnp.concatenate([io[d][2][:, sl], ones_blk], axis=1)
        wr3 = jnp.concatenate([st[i]["wr"]] * (ML_DH // ML_AUG + 1), axis=1)
        vw = (v_aug.astype(F32) * wr3).astype(mxd)
        c_refs[i][...] = st[i]["decay"] * c_refs[i][...] + _mm(io[d][1][sl, :], vw)
    m_ref[...] = jnp.concatenate([jnp.broadcast_to(c["m_new"], (1, m_all.shape[1])) for c in st], axis=0)


def _ml_out_body(hf_ref, hb_ref, z_ref, xc_ref, ng_ref, sk_ref, w_ref, x_ref, o_ref):
    @pl.when(pl.program_id(1) == 0)
    def _():
        o_ref[...] = x_ref[...]

    acc = o_ref[...]
    for hd in range(ML_OG):
        sl = pl.ds(hd * ML_DH, ML_DH)
        hh = hf_ref[:, sl].astype(F32) + hb_ref[:, sl].astype(F32)
        mu = jnp.mean(hh, axis=-1, keepdims=True)
        cen = hh - mu
        var = jnp.mean(cen * cen, axis=-1, keepdims=True)
        hn = cen * lax.rsqrt(var + EPS) * ng_ref[:, sl]
        out = _sigmoid(z_ref[:, sl].astype(F32)) * (hn + sk_ref[:, sl] * xc_ref[:, sl].astype(F32))
        acc = acc + _mm(out.astype(w_ref.dtype), w_ref[sl, :])
    o_ref[...] = acc


def _blockdiag(w):
    nb = ML_DH // ML_BLK
    wt = w.astype(F32).reshape(ML_HEADS, nb, ML_BLK, ML_BLK)
    eye = jnp.eye(nb, dtype=F32)
    return jnp.einsum("hncd,nm->hncmd", wt, eye).reshape(ML_HEADS, ML_DH, ML_DH).astype(_MXU_DTYPE)


def _mlstm_mixer(x, bsz, length, g, w_in, conv_w, conv_b, wq, wk, wv, w_gates, b_gates, norm_g, skip, w_out):
    t, d = x.shape
    mx = _MXU_DTYPE
    di, dh, nh = ML_INNER, ML_DH, ML_HEADS
    xz = _rms_mm(x, g, w_in, "ml_in")

    tm = _pick(length, 256)
    tiles_per_seq = length // tm
    hb = tm // HALO
    n_h = t // HALO
    wgt = w_gates.astype(mx).transpose(0, 2, 1)
    row_spec = lambda: pl.BlockSpec((tm, di), lambda i: (i, 0))
    small_t = lambda: pl.BlockSpec((ML_NGATE, tm), lambda i: (0, i))
    whole = lambda a: pl.BlockSpec(a.shape, lambda i: (0,) * a.ndim)
    consts = (conv_w.astype(F32), conv_b.astype(F32).reshape(1, di), _blockdiag(wq),
              _blockdiag(wk).transpose(0, 2, 1), _blockdiag(wv), wgt, b_gates.astype(F32).reshape(ML_NGATE, 1))
    outs = pl.pallas_call(
        functools.partial(_ml_pre_body, tiles_per_seq=tiles_per_seq),
        grid=(t // tm,),
        in_specs=[
            row_spec(),
            pl.BlockSpec((HALO, di), lambda i: (jnp.maximum(i * hb - 1, 0), 0)),
            pl.BlockSpec((HALO, di), lambda i: (jnp.minimum((i + 1) * hb, n_h - 1), 0)),
        ] + [whole(a) for a in consts],
        out_specs=[row_spec(), pl.BlockSpec((tm // ML_CHUNK, di, ML_CHUNK), lambda i: (i, 0, 0)), row_spec(),
                   row_spec(), small_t(), small_t(), small_t()],
        out_shape=[jax.ShapeDtypeStruct((t, di), mx), jax.ShapeDtypeStruct((t // ML_CHUNK, di, ML_CHUNK), mx)]
        + [jax.ShapeDtypeStruct((t, di), mx)] * 2 + [jax.ShapeDtypeStruct((ML_NGATE, t), F32)] * 3,
        scratch_shapes=[pltpu.VMEM((tm + 2 * HALO, di), F32)],
        compiler_params=_cparams(("parallel",)),
        name="ml_pre",
    )(xz, xz, xz, *consts)
    q, kt, v, xc, gates_t, pre_t, suf_t = outs
    rows = jnp.stack([pre_t[nh:2 * nh], suf_t[3 * nh:], gates_t[:nh], gates_t[2 * nh:3 * nh]], axis=1)
    cols = rows.reshape(nh // ML_HP, ML_HP, 4, t).transpose(0, 3, 1, 2).reshape(nh // ML_HP, t, ML_HP * 4)

    nc = length // ML_CHUNK
    hp = ML_HP
    fw = lambda b, h, j: (b * nc + j, h)
    bw = lambda b, h, j: (b * nc + nc - 1 - j, h)
    qkv = lambda f: pl.BlockSpec((ML_CHUNK, hp * dh), f)
    ktspec = lambda f: pl.BlockSpec((None, hp * dh, ML_CHUNK), lambda b, h, j: f(b, h, j) + (0,))
    rowspec = lambda f: pl.BlockSpec((hp, 4, ML_CHUNK), lambda b, h, j: (h, 0, f(b, h, j)[0]))
    colspec = lambda f: pl.BlockSpec((None, ML_CHUNK, hp * 4), lambda b, h, j: (h, f(b, h, j)[0], 0))
    hf, hbk = pl.pallas_call(
        _mlstm_body,
        grid=(bsz, nh // hp, nc),
        in_specs=[qkv(fw), ktspec(fw), qkv(fw), rowspec(fw), colspec(fw),
                  qkv(bw), ktspec(bw), qkv(bw), rowspec(bw), colspec(bw)],
        out_specs=[qkv(fw), qkv(bw)],
        out_shape=[jax.ShapeDtypeStruct((t, di), mx)] * 2,
        scratch_shapes=[pltpu.VMEM((2 * hp, 128), F32)] + [pltpu.VMEM((dh, dh + ML_AUG), F32)] * (2 * hp),
        compiler_params=_cparams(("parallel", "parallel", "arbitrary")),
        name="mlstm",
    )(q, kt, v, rows, cols, q, kt, v, rows, cols)

    tmo = _pick(t, 512)
    kg = ML_OG * dh
    n_g = nh // ML_OG
    grp = lambda: pl.BlockSpec((tmo, kg), lambda i, j: (i, j))
    return pl.pallas_call(
        _ml_out_body,
        grid=(t // tmo, n_g),
        in_specs=[
            grp(), grp(),
            pl.BlockSpec((tmo, kg), lambda i, j: (i, j + n_g)),
            grp(),
            pl.BlockSpec((1, kg), lambda i, j: (0, j)),
            pl.BlockSpec((1, kg), lambda i, j: (0, j)),
            pl.BlockSpec((kg, d), lambda i, j: (j, 0)),
            pl.BlockSpec((tmo, d), lambda i, j: (i, 0)),
        ],
        out_specs=pl.BlockSpec((tmo, d), lambda i, j: (i, 0)),
        out_shape=jax.ShapeDtypeStruct((t, d), F32),
        compiler_params=_cparams(("parallel", "arbitrary")),
        name="ml_out",
    )(hf, hbk, xz, xc, norm_g.astype(F32).reshape(1, di), skip.astype(F32).reshape(1, di), w_out, x)


def _trunk(x3, p):
    bsz, length, d = x3.shape
    x = x3.reshape(bsz * length, d)
    ffn = lambda x, layer, idx, **kw: _ffn(x, p["norm_g"][layer, 2 * idx], p["ffn_w_in"], p["ffn_w_out"],
                                           2 * layer + idx, **kw)
    x = ffn(x, 0, 0)
    x = _s5_mixer(x, bsz, length, p["norm_g"][0, 1], p["s5_w_in"], p["s5_tabs"], p["s5_w_glu"])
    x = ffn(x, 0, 1)
    x = ffn(x, 1, 0)
    x = _mlstm_mixer(x, bsz, length, p["norm_g"][1, 1], p["ml_w_in"], *p["ml_rest"])
    x = ffn(x, 1, 1, final_g=p["final_g"])
    return x.reshape(bsz, length, d)


def kernel(x_prompt, x_sample, norm_g, final_g, ffn_w_in, ffn_w_out, s5_w_in, s5_lambda_re, s5_lambda_im, s5_log_step, s5_b_re, s5_b_im, s5_c_re, s5_c_im, s5_d, s5_w_glu, ml_w_in, ml_conv_w, ml_conv_b, ml_wq, ml_wk, ml_wv, ml_w_gates, ml_b_gates, ml_norm_g, ml_skip, ml_w_out):
    mx = _MXU_DTYPE
    p = dict(
        norm_g=norm_g.astype(F32), final_g=final_g.astype(F32),
        ffn_w_in=ffn_w_in.astype(mx).reshape(-1, D_MODEL, 2 * D_FF),
        ffn_w_out=ffn_w_out.astype(mx).reshape(-1, D_FF, D_MODEL),
        s5_w_in=s5_w_in[0].astype(mx), s5_w_glu=s5_w_glu[0].astype(mx),
        s5_tabs=_s5_tables(s5_lambda_re[0], s5_lambda_im[0], s5_log_step[0], s5_b_re[0], s5_b_im[0], s5_c_re[0],
                           s5_c_im[0], s5_d[0]),
        ml_w_in=ml_w_in[0].astype(mx),
        ml_rest=(ml_conv_w[0], ml_conv_b[0], ml_wq[0], ml_wk[0], ml_wv[0], ml_w_gates[0], ml_b_gates[0],
                 ml_norm_g[0], ml_skip[0], ml_w_out[0].astype(mx)),
    )
    return (_trunk(x_prompt, p), _trunk(x_sample, p))
```

```python
import functools

import jax
import jax.numpy as jnp
from jax import lax
from jax.experimental import pallas as pl
from jax.experimental.pallas import tpu as pltpu

F32 = jnp.float32
_MXU_DTYPE = jnp.bfloat16

D_MODEL = 2048
D_FF = 5632
EPS = 1e-6
S5_GROUP = 16
S5_GROUPS = D_MODEL // S5_GROUP
S5_STATE = 64
S5_CHUNK = 16
S5_ROW = S5_CHUNK * S5_GROUP
S5_ST2 = 2 * S5_STATE
S5_GL = 8
S5_PANEL = 4
ML_INNER = 2 * D_MODEL
ML_HEADS = 16
ML_DH = ML_INNER // ML_HEADS
ML_BLK = 4
ML_CONV = 5
ML_CHUNK = 128
ML_NGATE = 4 * ML_HEADS
ML_HP = 4
ML_GRP = 2
ML_OG = 4
ML_AUG = 128
HALO = 16

_VMEM_LIMIT = 52 * 1024 * 1024
_VMEM_LIMIT_FFN = 60 * 1024 * 1024


def _cparams(sem, limit=_VMEM_LIMIT):
    return pltpu.CompilerParams(dimension_semantics=sem, vmem_limit_bytes=limit)


def _mm(a, b):
    return jnp.dot(a, b, preferred_element_type=F32)


def _mm_exact(a, b):
    return jnp.dot(a, b, preferred_element_type=F32, precision=lax.Precision.HIGHEST)


def _rms(x, g):
    ms = jnp.mean(x * x, axis=-1, keepdims=True)
    return x * lax.rsqrt(ms + EPS) * g


def _sigmoid(x):
    return 0.5 * jnp.tanh(0.5 * x) + 0.5


def _pick(n, pref):
    t = min(n, pref)
    while n % t:
        t //= 2
    return t


def _ffn_body(x_ref, g_ref, wg_ref, wu_ref, wo_ref, fg_ref, o_ref, hn_ref, *, n_j, final_norm):
    j = pl.program_id(1)

    @pl.when(j == 0)
    def _():
        hn_ref[...] = _rms(x_ref[...], g_ref[...]).astype(hn_ref.dtype)
        o_ref[...] = jnp.zeros_like(o_ref)

    h = hn_ref[...]
    gate = _mm(h, wg_ref[...])
    up = _mm(h, wu_ref[...])
    act = (gate * _sigmoid(gate) * up).astype(h.dtype)
    o_ref[...] += _mm(act, wo_ref[...])

    @pl.when(j == n_j - 1)
    def _():
        y = x_ref[...] + 0.5 * o_ref[...]
        if final_norm:
            y = _rms(y, fg_ref[...])
        o_ref[...] = y


def _ffn(x, g, w_in, w_out, k, final_g=None):
    t, d = x.shape
    tm = _pick(t, 1024 if final_g is None else 512)
    tn = 512
    n_j = D_FF // tn
    fg = g if final_g is None else final_g
    body = functools.partial(_ffn_body, n_j=n_j, final_norm=final_g is not None)
    return pl.pallas_call(
        body,
        grid=(t // tm, n_j),
        in_specs=[
            pl.BlockSpec((tm, d), lambda i, j: (i, 0)),
            pl.BlockSpec((1, d), lambda i, j: (0, 0)),
            pl.BlockSpec((None, d, tn), lambda i, j: (k, 0, j)),
            pl.BlockSpec((None, d, tn), lambda i, j: (k, 0, j + n_j)),
            pl.BlockSpec((None, tn, d), lambda i, j: (k, j, 0)),
            pl.BlockSpec((1, d), lambda i, j: (0, 0)),
        ],
        out_specs=pl.BlockSpec((tm, d), lambda i, j: (i, 0)),
        out_shape=jax.ShapeDtypeStruct((t, d), F32),
        scratch_shapes=[pltpu.VMEM((tm, d), _MXU_DTYPE)],
        compiler_params=_cparams(("parallel", "arbitrary"), _VMEM_LIMIT_FFN),
        name="ffn",
    )(x, g.reshape(1, d), w_in, w_in, w_out, fg.reshape(1, d))


def _rms_mm_body(x_ref, g_ref, w_ref, o_ref, hn_ref):
    @pl.when(pl.program_id(1) == 0)
    def _():
        hn_ref[...] = _rms(x_ref[...], g_ref[...]).astype(hn_ref.dtype)

    o_ref[...] = _mm(hn_ref[...], w_ref[...]).astype(o_ref.dtype)


def _rms_mm(x, g, w, name, out_dtype=None):
    t, d = x.shape
    n = w.shape[1]
    tm = _pick(t, 1024)
    tn = 1024
    return pl.pallas_call(
        _rms_mm_body,
        grid=(t // tm, n // tn),
        in_specs=[
            pl.BlockSpec((tm, d), lambda i, j: (i, 0)),
            pl.BlockSpec((1, d), lambda i, j: (0, 0)),
            pl.BlockSpec((d, tn), lambda i, j: (0, j)),
        ],
        out_specs=pl.BlockSpec((tm, tn), lambda i, j: (i, j)),
        out_shape=jax.ShapeDtypeStruct((t, n), out_dtype or _MXU_DTYPE),
        scratch_shapes=[pltpu.VMEM((tm, d), _MXU_DTYPE)],
        compiler_params=_cparams(("parallel", "arbitrary")),
        name=name,
    )(x, g.reshape(1, d), w)


def _s5_kern_body(cq_ref, pd_ref, o_ref):
    for n in range(cq_ref.shape[0]):
        o_ref[n] = _mm_exact(cq_ref[n], pd_ref[n])


def _s5_tables(lam_re, lam_im, log_step, b_re, b_im, c_re, c_im, d):
    g_n, p_n, gs, ck = S5_GROUPS, S5_STATE, S5_GROUP, S5_CHUNK
    lam = lax.complex(jnp.minimum(lam_re.astype(F32), -1e-4), lam_im.astype(F32))
    delta = jnp.exp(log_step.astype(F32))[..., None]
    lam_bar = jnp.exp(lam * delta)
    b_bar = ((lam_bar - 1.0) / lam)[..., None] * lax.complex(b_re.astype(F32), b_im.astype(F32))
    c = lax.complex(c_re.astype(F32), c_im.astype(F32))
    pows = [jnp.ones_like(lam_bar)]
    for _ in range(ck):
        pows.append(pows[-1] * lam_bar)
    pw = jnp.stack(pows)
    pw_rev = jnp.stack(pows[::-1])

    x = pw[:ck, :, :, :, None] * b_bar[None]
    xt = x.transpose(1, 2, 3, 0, 4).reshape(2, g_n, p_n, ck * gs)
    pd_t = jnp.concatenate([jnp.real(xt), jnp.imag(xt)], axis=2).reshape(2 * g_n, 2 * p_n, ck * gs)
    cq_t = jnp.concatenate([jnp.real(c), -jnp.imag(c)], axis=-1).reshape(2 * g_n, gs, 2 * p_n)
    gb = 8
    kt = pl.pallas_call(
        _s5_kern_body,
        grid=(2 * g_n // gb,),
        in_specs=[
            pl.BlockSpec((gb, gs, 2 * p_n), lambda i: (i, 0, 0)),
            pl.BlockSpec((gb, 2 * p_n, ck * gs), lambda i: (i, 0, 0)),
        ],
        out_specs=pl.BlockSpec((gb, gs, ck * gs), lambda i: (i, 0, 0)),
        out_shape=jax.ShapeDtypeStruct((2 * g_n, gs, ck * gs), F32),
        compiler_params=_cparams(("parallel",)),
        name="s5_kernels",
    )(cq_t, pd_t)
    kt = kt.reshape(2, g_n, gs, ck, gs)
    s_i = jnp.arange(ck)[:, None]
    t_i = jnp.arange(ck)[None, :]
    dfw = t_i - s_i
    dbw = s_i - t_i
    tf = jnp.where((dfw >= 0)[None, None, :, :, None], kt[0][:, :, jnp.clip(dfw, 0, ck - 1), :], 0.0)
    tb = jnp.where((dbw >= 0)[None, None, :, :, None], kt[1][:, :, jnp.clip(dbw, 0, ck - 1), :], 0.0)
    toep = (tf + tb).transpose(0, 2, 4, 3, 1).reshape(g_n, ck * gs, ck * gs)

    def reim_last(z):
        return jnp.concatenate([jnp.real(z), jnp.imag(z)], axis=-1)

    xf = pw_rev[1:ck + 1, 0, :, :, None] * b_bar[0][None]
    xf = xf.transpose(1, 0, 3, 2).reshape(g_n, ck * gs, p_n)
    xb = x[:, 1].transpose(1, 0, 3, 2).reshape(g_n, ck * gs, p_n)
    w_state = jnp.concatenate([reim_last(xf), reim_last(xb)], axis=-1)

    zf = (c[0][None] * pw[1:ck + 1, 0][:, :, None, :]).transpose(1, 3, 0, 2).reshape(g_n, p_n, ck * gs)
    zb = (c[1][None] * pw_rev[:ck, 1][:, :, None, :]).transpose(1, 3, 0, 2).reshape(g_n, p_n, ck * gs)
    wo_f = jnp.concatenate([jnp.real(zf), -jnp.imag(zf)], axis=1)
    wo_b = jnp.concatenate([jnp.real(zb), -jnp.imag(zb)], axis=1)

    def coef(a):
        return (jnp.concatenate([jnp.real(a), jnp.real(a)], -1), jnp.concatenate([-jnp.imag(a), jnp.imag(a)], -1))

    a_f = coef(pw[ck, 0])
    a_b = coef(pw[ck, 1])
    mx = _MXU_DTYPE
    n8, gl = g_n // S5_GL, S5_GL
    kk = ck * gl * gs

    def rows_sgc(w):
        n = w.shape[-1]
        return w.astype(mx).reshape(n8, gl, ck, gs, n).transpose(0, 2, 1, 3, 4).reshape(n8, kk, n)

    def rows_gp(w):
        return w.astype(mx).reshape(n8, gl * w.shape[1], w.shape[2])

    td = (ck, gs)
    ep = (2, 2 * p_n)
    return dict(toep=_s5_widen(rows_sgc(toep), gs, td), w_state=_s5_widen(rows_sgc(w_state), gs, ep),
                wo_f=_s5_widen(rows_gp(wo_f), 2 * p_n, td), wo_b=_s5_widen(rows_gp(wo_b), 2 * p_n, td),
                a_f=a_f, a_b=a_b, d=d.astype(F32).reshape(1, g_n * gs))


def _s5_widen_body(w_ref, e_ref, o_ref, *, row_div, col_div):
    y = _mm(w_ref[0], e_ref[...])
    r_g = (lax.broadcasted_iota(jnp.int32, y.shape, 0) // row_div) % S5_GL
    c_g = ((lax.broadcasted_iota(jnp.int32, y.shape, 1) + pl.program_id(1) * y.shape[1]) // col_div) % S5_GL
    o_ref[0] = jnp.where(r_g == c_g, y, 0.0).astype(o_ref.dtype)


def _s5_widen(w, row_div, col_dims):
    n8, rows, n = w.shape
    a, b = col_dims
    e = jnp.broadcast_to(jnp.eye(a * b, dtype=w.dtype).reshape(a, b, a, 1, b), (a, b, a, S5_GL, b))
    e = e.reshape(n, n * S5_GL)
    tn = 512
    return pl.pallas_call(
        functools.partial(_s5_widen_body, row_div=row_div, col_div=b),
        grid=(n8, n * S5_GL // tn),
        in_specs=[pl.BlockSpec((1, rows, n), lambda x, j: (x, 0, 0)), pl.BlockSpec((n, tn), lambda x, j: (0, j))],
        out_specs=pl.BlockSpec((1, rows, tn), lambda x, j: (x, 0, j)),
        out_shape=jax.ShapeDtypeStruct((n8, rows, n * S5_GL), w.dtype),
        compiler_params=_cparams(("parallel", "arbitrary")),
        name="s5_widen",
    )(w, e)


def _s5_gather_chunks(x_ref, xc_ref):
    rt = xc_ref.shape[0]
    parts = [x_ref[pl.ds(s, rt, stride=S5_CHUNK), :] for s in range(S5_CHUNK)]
    xc_ref[...] = jnp.concatenate(parts, axis=1).astype(xc_ref.dtype)


def _s5_state_in_body(u_ref, w_ref, vf_ref, vb_ref, xc_ref, pf_ref, pb_ref):
    _s5_gather_chunks(u_ref, xc_ref)
    rt = xc_ref.shape[0]
    v = _mm(xc_ref[...], w_ref[0])
    for n, (o_ref, p_ref) in enumerate(((vf_ref, pf_ref), (vb_ref, pb_ref))):
        for gi in range(S5_GL):
            c0 = (n * S5_GL + gi) * S5_ST2
            p_ref[pl.ds(gi, rt, stride=S5_GL), :] = v[:, c0:c0 + S5_ST2]
        o_ref[...] = p_ref[...].reshape(o_ref.shape)


def _s5_scan_body(vf_ref, vb_ref, afr_ref, afi_ref, abr_ref, abi_ref, sf_ref, sb_ref, cf_ref, cb_ref, *, rt):
    @pl.when(pl.program_id(1) == 0)
    def _():
        cf_ref[...] = jnp.zeros_like(cf_ref)
        cb_ref[...] = jnp.zeros_like(cb_ref)

    afr, afi, abr, abi = afr_ref[...], afi_ref[...], abr_ref[...], abi_ref[...]

    def step(k, carry):
        s_f, s_b = carry
        sf_ref[0, k] = s_f.astype(sf_ref.dtype)
        s_f = afr * s_f + afi * pltpu.roll(s_f, S5_STATE, 1) + vf_ref[0, k]
        kb = rt - 1 - k
        sb_ref[0, kb] = s_b.astype(sb_ref.dtype)
        s_b = abr * s_b + abi * pltpu.roll(s_b, S5_STATE, 1) + vb_ref[0, kb]
        return s_f, s_b

    s_f, s_b = lax.fori_loop(0, rt, step, (cf_ref[...], cb_ref[...]))
    cf_ref[...] = s_f
    cb_ref[...] = s_b


def _s5_out_body(u_ref, t_ref, sf_ref, sb_ref, wof_ref, wob_ref, d_ref, o_ref, xc_ref, pf_ref, pb_ref):
    _s5_gather_chunks(u_ref, xc_ref)
    rt = xc_ref.shape[0]
    mxd = xc_ref.dtype

    def by_group(s_ref, p_ref):
        p_ref[...] = s_ref[...].reshape(p_ref.shape)
        return jnp.concatenate([p_ref[pl.ds(gi, rt, stride=S5_GL), :] for gi in range(S5_GL)], axis=1).astype(mxd)

    s_f = by_group(sf_ref, pf_ref)
    s_b = by_group(sb_ref, pb_ref)
    xc = xc_ref[...]
    pw = S5_PANEL * 128
    for p0 in range(0, S5_CHUNK, S5_PANEL):
        cols = pl.ds(p0 * 128, pw)
        y = _mm(xc, t_ref[0, :, cols]) + _mm(s_f, wof_ref[0, :, cols]) + _mm(s_b, wob_ref[0, :, cols])
        for s in range(S5_PANEL):
            rows = pl.ds(p0 + s, rt, stride=S5_CHUNK)
            o_ref[rows, :] = jax.nn.gelu(y[:, s * 128:(s + 1) * 128] + d_ref[...] * u_ref[rows, :])


def _glu_body(y_ref, wv_ref, wg_ref, x_ref, o_ref, yb_ref):
    @pl.when(pl.program_id(1) == 0)
    def _():
        yb_ref[...] = y_ref[...].astype(yb_ref.dtype)

    y = yb_ref[...]
    val = _mm(y, wv_ref[...])
    gate = _mm(y, wg_ref[...])
    o_ref[...] = x_ref[...] + val * _sigmoid(gate)


def _s5_mixer(x, bsz, length, g, w_in, tabs, w_glu):
    t, d = x.shape
    g_n, ck = S5_GROUPS, S5_CHUNK
    r_n = length // ck
    rows = r_n * bsz
    mx = _MXU_DTYPE
    n8 = g_n // S5_GL
    lanes = S5_GL * S5_GROUP
    k8 = ck * lanes
    st8 = S5_GL * S5_ST2
    u = _rms_mm(x, g, w_in, "s5_in", out_dtype=F32)

    rb = _pick(rows, 256)
    u_spec = pl.BlockSpec((rb * ck, lanes), lambda j, i: (i, j))
    st_spec = lambda: pl.BlockSpec((rb, S5_GL, S5_ST2), lambda j, i: (i, j, 0))
    held = lambda shape: pl.BlockSpec((1,) + shape, lambda j, i: (j, 0, 0))
    vf, vb = pl.pallas_call(
        _s5_state_in_body,
        grid=(n8, rows // rb),
        in_specs=[u_spec, held((k8, 2 * st8))],
        out_specs=[st_spec(), st_spec()],
        out_shape=[jax.ShapeDtypeStruct((rows, g_n, S5_ST2), F32)] * 2,
        scratch_shapes=[pltpu.VMEM((rb, k8), mx)] + [pltpu.VMEM((rb * S5_GL, S5_ST2), F32)] * 2,
        compiler_params=_cparams(("parallel", "arbitrary")),
        name="s5_state_in",
    )(u, tabs["w_state"])

    rt = _pick(r_n, 16)
    n_rt = r_n // rt
    v4 = (bsz, r_n, g_n, S5_ST2)
    blk = (1, rt, g_n, S5_ST2)
    fwd = lambda b, i: (b, i, 0, 0)
    bwd = lambda b, i: (b, n_rt - 1 - i, 0, 0)
    coef_spec = pl.BlockSpec((g_n, S5_ST2), lambda b, i: (0, 0))
    sf, sb = pl.pallas_call(
        functools.partial(_s5_scan_body, rt=rt),
        grid=(bsz, n_rt),
        in_specs=[pl.BlockSpec(blk, fwd), pl.BlockSpec(blk, bwd), coef_spec, coef_spec, coef_spec, coef_spec],
        out_specs=[pl.BlockSpec(blk, fwd), pl.BlockSpec(blk, bwd)],
        out_shape=[jax.ShapeDtypeStruct(v4, F32)] * 2,
        scratch_shapes=[pltpu.VMEM((g_n, S5_ST2), F32)] * 2,
        compiler_params=_cparams(("parallel", "arbitrary")),
        name="s5_scan",
    )(vf.reshape(v4), vb.reshape(v4), tabs["a_f"][0], tabs["a_f"][1], tabs["a_b"][0], tabs["a_b"][1])
    sf = sf.reshape(rows, g_n, S5_ST2)
    sb = sb.reshape(rows, g_n, S5_ST2)

    y = pl.pallas_call(
        _s5_out_body,
        grid=(n8, rows // rb),
        in_specs=[u_spec, held((k8, k8)), st_spec(), st_spec(), held((st8, k8)), held((st8, k8)),
                  pl.BlockSpec((1, lanes), lambda j, i: (0, j))],
        out_specs=u_spec,
        out_shape=jax.ShapeDtypeStruct((t, d), F32),
        scratch_shapes=[pltpu.VMEM((rb, k8), mx)] + [pltpu.VMEM((rb * S5_GL, S5_ST2), F32)] * 2,
        compiler_params=_cparams(("parallel", "arbitrary")),
        name="s5_out",
    )(u, tabs["toep"], sf, sb, tabs["wo_f"], tabs["wo_b"], tabs["d"])

    tm = _pick(t, 1024)
    tn = 512
    n_j = d // tn
    return pl.pallas_call(
        _glu_body,
        grid=(t // tm, n_j),
        in_specs=[
            pl.BlockSpec((tm, d), lambda i, j: (i, 0)),
            pl.BlockSpec((d, tn), lambda i, j: (0, j)),
            pl.BlockSpec((d, tn), lambda i, j: (0, j + n_j)),
            pl.BlockSpec((tm, tn), lambda i, j: (i, j)),
        ],
        out_specs=pl.BlockSpec((tm, tn), lambda i, j: (i, j)),
        out_shape=jax.ShapeDtypeStruct((t, d), F32),
        scratch_shapes=[pltpu.VMEM((tm, d), mx)],
        compiler_params=_cparams(("parallel", "arbitrary")),
        name="s5_glu",
    )(y, w_glu, w_glu, x)


def _log_sigmoid(x):
    return jnp.minimum(x, 0.0) - jnp.log(1.0 + jnp.exp(-jnp.abs(x)))


def _ml_pre_body(xm_ref, prev_ref, next_ref, cw_ref, cb_ref, bdq_ref, bdkt_ref, bdv_ref, wgt_ref, bgt_ref,
                 qv_ref, kt_ref, xc_ref, gatest_ref, pt_ref, st_ref, ext_ref, *, tiles_per_seq):
    tm = xm_ref.shape[0]
    mxd = xm_ref.dtype
    pos = lax.rem(pl.program_id(0), tiles_per_seq)
    ext_ref[pl.ds(HALO, tm), :] = xm_ref[...].astype(F32)
    ext_ref[pl.ds(0, HALO), :] = jnp.where(pos == 0, 0.0, prev_ref[...].astype(F32))
    ext_ref[pl.ds(HALO + tm, HALO), :] = jnp.where(pos == tiles_per_seq - 1, 0.0, next_ref[...].astype(F32))
    nt = (((1,), (1,)), ((), ()))
    scale = ML_DH ** -0.5

    def conv(h):
        sl = pl.ds(h * ML_DH, ML_DH)
        pre = cb_ref[:, sl]
        for jj in range(ML_CONV):
            pre = pre + cw_ref[jj:jj + 1, sl] * ext_ref[pl.ds(HALO - ML_CONV // 2 + jj, tm), sl]
        xc = (pre * _sigmoid(pre)).astype(mxd)
        xc_ref[:, sl] = xc
        return xc

    def project(h, xc):
        sl = pl.ds(h * ML_DH, ML_DH)
        qh = _mm(xc, bdq_ref[h]).astype(mxd)
        vh = _mm(xm_ref[:, sl], bdv_ref[h]).astype(mxd)
        kt = (lax.dot_general(bdkt_ref[h], xc, nt, preferred_element_type=F32) * scale).astype(mxd)
        c0 = (h // ML_HP) * 2 * ML_HP * ML_DH + (h % ML_HP) * ML_DH
        qv_ref[:, pl.ds(c0, ML_DH)] = qh
        qv_ref[:, pl.ds(c0 + ML_HP * ML_DH, ML_DH)] = vh
        for c in range(tm // ML_CHUNK):
            kt_ref[c, sl, :] = kt[:, c * ML_CHUNK:(c + 1) * ML_CHUNK]
        return (lax.dot_general(wgt_ref[0, :, sl], qh, nt, preferred_element_type=F32)
                + _mm(wgt_ref[1, :, sl], kt) * (1.0 / scale)
                + lax.dot_general(wgt_ref[2, :, sl], vh, nt, preferred_element_type=F32))

    gates_t = jnp.broadcast_to(bgt_ref[...], (ML_NGATE, tm))
    xc_prev = conv(0)
    for h in range(1, ML_HEADS):
        xc_next = conv(h)
        gates_t = gates_t + project(h - 1, xc_prev)
        xc_prev = xc_next
    gates_t = gates_t + project(ML_HEADS - 1, xc_prev)

    r_i = lax.broadcasted_iota(jnp.int32, (ML_CHUNK, ML_CHUNK), 0)
    c_i = lax.broadcasted_iota(jnp.int32, (ML_CHUNK, ML_CHUNK), 1)
    incl_before = (r_i <= c_i).astype(F32)
    incl_after = (r_i >= c_i).astype(F32)
    gatest_ref[...] = gates_t
    for c in range(tm // ML_CHUNK):
        sl = pl.ds(c * ML_CHUNK, ML_CHUNK)
        lst = _log_sigmoid(gates_t[:, c * ML_CHUNK:(c + 1) * ML_CHUNK])
        pt_ref[:, sl] = _mm_exact(lst, incl_before)
        st_ref[:, sl] = _mm_exact(lst, incl_after)


def _mlstm_body(qvf, kf, rowf, colf, qvb, kb, rowb, colb, hf_ref, hb_ref, m_ref, *c_refs):
    @pl.when(pl.program_id(2) == 0)
    def _():
        m_ref[...] = jnp.zeros_like(m_ref)
        for c_ref in c_refs:
            c_ref[...] = jnp.zeros_like(c_ref)

    qv = (qvf, qvb)
    q_of = lambda d, hd: qv[d][:, pl.ds(hd * ML_DH, ML_DH)]
    v_of = lambda d, hd: qv[d][:, pl.ds((ML_HP + hd) * ML_DH, ML_DH)]
    io = ((None, kf, None, rowf, hf_ref, colf), (None, kb, None, rowb, hb_ref, colb))
    chains = [(hd, d) for hd in range(ML_HP) for d in (0, 1)]
    t_i = lax.broadcasted_iota(jnp.int32, (ML_CHUNK, ML_CHUNK), 0)
    s_i = lax.broadcasted_iota(jnp.int32, (ML_CHUNK, ML_CHUNK), 1)
    masks = (s_i <= t_i, s_i >= t_i)
    m_all = m_ref[...]
    mxd = qvf.dtype
    sq = (ML_CHUNK, ML_AUG)
    ones_blk = jnp.ones(sq, mxd)

    st = {}

    def run_stages(sub):
        for i, (hd, d) in sub:
            row = io[d][3][hd]
            col = io[d][5][:, hd * 4:(hd + 1) * 4]
            bc = col[:, d:d + 1]
            lic = col[:, 2 + d:3 + d]
            bcr = row[d:d + 1, :]
            lir = row[2 + d:3 + d, :]
            g = bcr[:, ML_CHUNK - 1:ML_CHUNK] if d == 0 else bcr[:, 0:1]
            m_prev = m_all[i:i + 1, 0:1]
            bc_b = jnp.broadcast_to(bc, sq)
            a_b = bc_b + m_prev
            drow = lir - bcr
            dm = jnp.where(masks[d], bc_b + drow, -jnp.inf)
            m_t = jnp.maximum(a_b, jnp.max(dm, axis=1, keepdims=True))
            dec = jnp.exp(dm - m_t)
            e = jnp.exp(a_b - m_t)
            m_new = jnp.maximum(g + m_prev, jnp.max(g + drow, axis=1, keepdims=True))
            decay = jnp.exp(g + m_prev - m_new)
            wr = jnp.exp(g - bc_b + jnp.broadcast_to(lic, sq) - m_new)
            st[i] = dict(m_t=m_t, dec=dec, e=e, m_new=m_new, decay=decay, wr=wr)

        for i, (hd, d) in sub:
            sl = pl.ds(hd * ML_DH, ML_DH)
            s = _mm(q_of(d, hd), io[d][1][sl, :]) * st[i]["dec"]
            st[i]["s"] = s.astype(mxd)

        for i, (hd, d) in sub:
            sl = pl.ds(hd * ML_DH, ML_DH)
            v_aug = jnp.concatenate([v_of(d, hd), ones_blk], axis=1)
            e3 = jnp.concatenate([st[i]["e"]] * (ML_DH // ML_AUG + 1), axis=1)
            nd = e3 * _mm(q_of(d, hd), c_refs[i][...].astype(mxd)) + _mm(st[i]["s"], v_aug)
            inv = 1.0 / jnp.maximum(jnp.abs(nd[:, ML_DH:]), jnp.exp(-st[i]["m_t"]))
            inv2 = jnp.concatenate([inv] * (ML_DH // ML_AUG), axis=1)
            io[d][4][:, sl] = (nd[:, :ML_DH] * inv2).astype(io[d][4].dtype)

        for i, (hd, d) in sub:
            sl = pl.ds(hd * ML_DH, ML_DH)
            v_aug = jnp.concatenate([v_of(d, hd), ones_blk], axis=1)
            wr3 = jnp.concatenate([st[i]["wr"]] * (ML_DH // ML_AUG + 1), axis=1)
            vw = (v_aug.astype(F32) * wr3).astype(mxd)
            c_refs[i][...] = st[i]["decay"] * c_refs[i][...] + _mm(io[d][1][sl, :], vw)

    for g0 in range(0, len(chains), ML_GRP):
        run_stages(list(enumerate(chains))[g0:g0 + ML_GRP])
    m_ref[...] = jnp.concatenate([jnp.broadcast_to(st[i]["m_new"], (1, m_all.shape[1]))
                                  for i in range(len(chains))], axis=0)


def _ml_out_body(hf_ref, hb_ref, z_ref, xc_ref, ng_ref, sk_ref, w_ref, x_ref, o_ref):
    @pl.when(pl.program_id(1) == 0)
    def _():
        o_ref[...] = x_ref[...]

    acc = o_ref[...]
    for hd in range(ML_OG):
        sl = pl.ds(hd * ML_DH, ML_DH)
        hh = hf_ref[:, sl].astype(F32) + hb_ref[:, sl].astype(F32)
        mu = jnp.mean(hh, axis=-1, keepdims=True)
        cen = hh - mu
        var = jnp.mean(cen * cen, axis=-1, keepdims=True)
        hn = cen * lax.rsqrt(var + EPS) * ng_ref[:, sl]
        out = _sigmoid(z_ref[:, sl].astype(F32)) * (hn + sk_ref[:, sl] * xc_ref[:, sl].astype(F32))
        acc = acc + _mm(out.astype(w_ref.dtype), w_ref[sl, :])
    o_ref[...] = acc


def _blockdiag(w):
    nb = ML_DH // ML_BLK
    wt = w.astype(F32).reshape(ML_HEADS, nb, ML_BLK, ML_BLK)
    eye = jnp.eye(nb, dtype=F32)
    return jnp.einsum("hncd,nm->hncmd", wt, eye).reshape(ML_HEADS, ML_DH, ML_DH).astype(_MXU_DTYPE)


def _mlstm_mixer(x, bsz, length, g, w_in, conv_w, conv_b, wq, wk, wv, w_gates, b_gates, norm_g, skip, w_out):
    t, d = x.shape
    mx = _MXU_DTYPE
    di, dh, nh = ML_INNER, ML_DH, ML_HEADS
    xz = _rms_mm(x, g, w_in, "ml_in")

    tm = _pick(length, 256)
    tiles_per_seq = length // tm
    hb = tm // HALO
    n_h = t // HALO
    wgt = w_gates.astype(mx).transpose(0, 2, 1)
    row_spec = lambda: pl.BlockSpec((tm, di), lambda i: (i, 0))
    small_t = lambda: pl.BlockSpec((ML_NGATE, tm), lambda i: (0, i))
    whole = lambda a: pl.BlockSpec(a.shape, lambda i: (0,) * a.ndim)
    consts = (conv_w.astype(F32), conv_b.astype(F32).reshape(1, di), _blockdiag(wq),
              _blockdiag(wk).transpose(0, 2, 1), _blockdiag(wv), wgt, b_gates.astype(F32).reshape(ML_NGATE, 1))
    outs = pl.pallas_call(
        functools.partial(_ml_pre_body, tiles_per_seq=tiles_per_seq),
        grid=(t // tm,),
        in_specs=[
            row_spec(),
            pl.BlockSpec((HALO, di), lambda i: (jnp.maximum(i * hb - 1, 0), 0)),
            pl.BlockSpec((HALO, di), lambda i: (jnp.minimum((i + 1) * hb, n_h - 1), 0)),
        ] + [whole(a) for a in consts],
        out_specs=[pl.BlockSpec((tm, 2 * di), lambda i: (i, 0)),
                   pl.BlockSpec((tm // ML_CHUNK, di, ML_CHUNK), lambda i: (i, 0, 0)),
                   row_spec(), small_t(), small_t(), small_t()],
        out_shape=[jax.ShapeDtypeStruct((t, 2 * di), mx), jax.ShapeDtypeStruct((t // ML_CHUNK, di, ML_CHUNK), mx),
                   jax.ShapeDtypeStruct((t, di), mx)] + [jax.ShapeDtypeStruct((ML_NGATE, t), F32)] * 3,
        scratch_shapes=[pltpu.VMEM((tm + 2 * HALO, di), F32)],
        compiler_params=_cparams(("parallel",)),
        name="ml_pre",
    )(xz, xz, xz, *consts)
    qv, kt, xc, gates_t, pre_t, suf_t = outs
    rows = jnp.stack([pre_t[nh:2 * nh], suf_t[3 * nh:], gates_t[:nh], gates_t[2 * nh:3 * nh]], axis=1)
    cols = rows.reshape(nh // ML_HP, ML_HP, 4, t).transpose(0, 3, 1, 2).reshape(nh // ML_HP, t, ML_HP * 4)

    nc = length // ML_CHUNK
    hp = ML_HP
    fw = lambda b, h, j: (b * nc + j, h)
    bw = lambda b, h, j: (b * nc + nc - 1 - j, h)
    qkv = lambda f: pl.BlockSpec((ML_CHUNK, hp * dh), f)
    qvspec = lambda f: pl.BlockSpec((ML_CHUNK, 2 * hp * dh), f)
    ktspec = lambda f: pl.BlockSpec((None, hp * dh, ML_CHUNK), lambda b, h, j: f(b, h, j) + (0,))
    rowspec = lambda f: pl.BlockSpec((hp, 4, ML_CHUNK), lambda b, h, j: (h, 0, f(b, h, j)[0]))
    colspec = lambda f: pl.BlockSpec((None, ML_CHUNK, hp * 4), lambda b, h, j: (h, f(b, h, j)[0], 0))
    hf, hbk = pl.pallas_call(
        _mlstm_body,
        grid=(bsz, nh // hp, nc),
        in_specs=[qvspec(fw), ktspec(fw), rowspec(fw), colspec(fw),
                  qvspec(bw), ktspec(bw), rowspec(bw), colspec(bw)],
        out_specs=[qkv(fw), qkv(bw)],
        out_shape=[jax.ShapeDtypeStruct((t, di), mx)] * 2,
        scratch_shapes=[pltpu.VMEM((2 * hp, 128), F32)] + [pltpu.VMEM((dh, dh + ML_AUG), F32)] * (2 * hp),
        compiler_params=_cparams(("parallel", "parallel", "arbitrary")),
        name="mlstm",
    )(qv, kt, rows, cols, qv, kt, rows, cols)

    tmo = _pick(t, 512)
    kg = ML_OG * dh
    n_g = nh // ML_OG
    grp = lambda: pl.BlockSpec((tmo, kg), lambda i, j: (i, j))
    return pl.pallas_call(
        _ml_out_body,
        grid=(t // tmo, n_g),
        in_specs=[
            grp(), grp(),
            pl.BlockSpec((tmo, kg), lambda i, j: (i, j + n_g)),
            grp(),
            pl.BlockSpec((1, kg), lambda i, j: (0, j)),
            pl.BlockSpec((1, kg), lambda i, j: (0, j)),
            pl.BlockSpec((kg, d), lambda i, j: (j, 0)),
            pl.BlockSpec((tmo, d), lambda i, j: (i, 0)),
        ],
        out_specs=pl.BlockSpec((tmo, d), lambda i, j: (i, 0)),
        out_shape=jax.ShapeDtypeStruct((t, d), F32),
        compiler_params=_cparams(("parallel", "arbitrary")),
        name="ml_out",
    )(hf, hbk, xz, xc, norm_g.astype(F32).reshape(1, di), skip.astype(F32).reshape(1, di), w_out, x)


def _trunk(x3, p):
    bsz, length, d = x3.shape
    x = x3.reshape(bsz * length, d)
    ffn = lambda x, layer, idx, **kw: _ffn(x, p["norm_g"][layer, 2 * idx], p["ffn_w_in"], p["ffn_w_out"],
                                           2 * layer + idx, **kw)
    x = ffn(x, 0, 0)
    x = _s5_mixer(x, bsz, length, p["norm_g"][0, 1], p["s5_w_in"], p["s5_tabs"], p["s5_w_glu"])
    x = ffn(x, 0, 1)
    x = ffn(x, 1, 0)
    x = _mlstm_mixer(x, bsz, length, p["norm_g"][1, 1], p["ml_w_in"], *p["ml_rest"])
    x = ffn(x, 1, 1, final_g=p["final_g"])
    return x.reshape(bsz, length, d)


def kernel(x_prompt, x_sample, norm_g, final_g, ffn_w_in, ffn_w_out, s5_w_in, s5_lambda_re, s5_lambda_im, s5_log_step, s5_b_re, s5_b_im, s5_c_re, s5_c_im, s5_d, s5_w_glu, ml_w_in, ml_conv_w, ml_conv_b, ml_wq, ml_wk, ml_wv, ml_w_gates, ml_b_gates, ml_norm_g, ml_skip, ml_w_out):
    mx = _MXU_DTYPE
    p = dict(
        norm_g=norm_g.astype(F32), final_g=final_g.astype(F32),
        ffn_w_in=ffn_w_in.astype(mx).reshape(-1, D_MODEL, 2 * D_FF),
        ffn_w_out=ffn_w_out.astype(mx).reshape(-1, D_FF, D_MODEL),
        s5_w_in=s5_w_in[0].astype(mx), s5_w_glu=s5_w_glu[0].astype(mx),
        s5_tabs=_s5_tables(s5_lambda_re[0], s5_lambda_im[0], s5_log_step[0], s5_b_re[0], s5_b_im[0], s5_c_re[0],
                           s5_c_im[0], s5_d[0]),
        ml_w_in=ml_w_in[0].astype(mx),
        ml_rest=(ml_conv_w[0], ml_conv_b[0], ml_wq[0], ml_wk[0], ml_wv[0], ml_w_gates[0], ml_b_gates[0],
                 ml_norm_g[0], ml_skip[0], ml_w_out[0].astype(mx)),
    )
    return (_trunk(x_prompt, p), _trunk(x_sample, p))
```

```python
import functools

import jax
import jax.numpy as jnp
from jax import lax
from jax.experimental import pallas as pl
from jax.experimental.pallas import tpu as pltpu

F32 = jnp.float32
_MXU_DTYPE = jnp.bfloat16

D_MODEL = 2048
D_FF = 5632
EPS = 1e-6
S5_GROUP = 16
S5_GROUPS = D_MODEL // S5_GROUP
S5_STATE = 64
S5_CHUNK = 16
S5_ROW = S5_CHUNK * S5_GROUP
S5_ST2 = 2 * S5_STATE
S5_GL = 8
S5_PANEL = 4
ML_INNER = 2 * D_MODEL
ML_HEADS = 16
ML_DH = ML_INNER // ML_HEADS
ML_BLK = 4
ML_CONV = 5
ML_CHUNK = 128
ML_NGATE = 4 * ML_HEADS
ML_HP = 4
ML_GRP = 2
ML_OG = 4
ML_AUG = 128
HALO = 16

_VMEM_LIMIT = 52 * 1024 * 1024
_VMEM_LIMIT_FFN = 60 * 1024 * 1024


def _cparams(sem, limit=_VMEM_LIMIT):
    return pltpu.CompilerParams(dimension_semantics=sem, vmem_limit_bytes=limit)


def _mm(a, b):
    return jnp.dot(a, b, preferred_element_type=F32)


def _mm_exact(a, b):
    return jnp.dot(a, b, preferred_element_type=F32, precision=lax.Precision.HIGHEST)


def _rms(x, g):
    ms = jnp.mean(x * x, axis=-1, keepdims=True)
    return x * lax.rsqrt(ms + EPS) * g


def _sigmoid(x):
    return 0.5 * jnp.tanh(0.5 * x) + 0.5


def _pick(n, pref):
    t = min(n, pref)
    while n % t:
        t //= 2
    return t


def _ffn_body(x_ref, g_ref, wg_ref, wu_ref, wo_ref, fg_ref, o_ref, hn_ref, *, n_j, final_norm):
    j = pl.program_id(1)

    @pl.when(j == 0)
    def _():
        hn_ref[...] = _rms(x_ref[...], g_ref[...]).astype(hn_ref.dtype)
        o_ref[...] = jnp.zeros_like(o_ref)

    h = hn_ref[...]
    gate = _mm(h, wg_ref[...])
    up = _mm(h, wu_ref[...])
    act = (gate * _sigmoid(gate) * up).astype(h.dtype)
    o_ref[...] += _mm(act, wo_ref[...])

    @pl.when(j == n_j - 1)
    def _():
        rb = min(o_ref.shape[0], 128)

        def block(r, carry):
            rows = pl.ds(pl.multiple_of(r * rb, rb), rb)
            y = x_ref[rows, :] + 0.5 * o_ref[rows, :]
            if final_norm:
                y = _rms(y, fg_ref[...])
            o_ref[rows, :] = y
            return carry

        lax.fori_loop(0, o_ref.shape[0] // rb, block, 0)


def _ffn(x, g, w_in, w_out, k, final_g=None):
    t, d = x.shape
    tm = _pick(t, 1024)
    tn = 512
    n_j = D_FF // tn
    fg = g if final_g is None else final_g
    body = functools.partial(_ffn_body, n_j=n_j, final_norm=final_g is not None)
    return pl.pallas_call(
        body,
        grid=(t // tm, n_j),
        in_specs=[
            pl.BlockSpec((tm, d), lambda i, j: (i, 0)),
            pl.BlockSpec((1, d), lambda i, j: (0, 0)),
            pl.BlockSpec((None, d, tn), lambda i, j: (k, 0, j)),
            pl.BlockSpec((None, d, tn), lambda i, j: (k, 0, j + n_j)),
            pl.BlockSpec((None, tn, d), lambda i, j: (k, j, 0)),
            pl.BlockSpec((1, d), lambda i, j: (0, 0)),
        ],
        out_specs=pl.BlockSpec((tm, d), lambda i, j: (i, 0)),
        out_shape=jax.ShapeDtypeStruct((t, d), F32),
        scratch_shapes=[pltpu.VMEM((tm, d), _MXU_DTYPE)],
        compiler_params=_cparams(("parallel", "arbitrary"), _VMEM_LIMIT_FFN),
        name="ffn",
    )(x, g.reshape(1, d), w_in, w_in, w_out, fg.reshape(1, d))


def _rms_mm_body(x_ref, g_ref, w_ref, o_ref, hn_ref):
    @pl.when(pl.program_id(1) == 0)
    def _():
        hn_ref[...] = _rms(x_ref[...], g_ref[...]).astype(hn_ref.dtype)

    o_ref[...] = _mm(hn_ref[...], w_ref[...]).astype(o_ref.dtype)


def _rms_mm(x, g, w, name, out_dtype=None):
    t, d = x.shape
    n = w.shape[1]
    tm = _pick(t, 1024)
    tn = 1024
    return pl.pallas_call(
        _rms_mm_body,
        grid=(t // tm, n // tn),
        in_specs=[
            pl.BlockSpec((tm, d), lambda i, j: (i, 0)),
            pl.BlockSpec((1, d), lambda i, j: (0, 0)),
            pl.BlockSpec((d, tn), lambda i, j: (0, j)),
        ],
        out_specs=pl.BlockSpec((tm, tn), lambda i, j: (i, j)),
        out_shape=jax.ShapeDtypeStruct((t, n), out_dtype or _MXU_DTYPE),
        scratch_shapes=[pltpu.VMEM((tm, d), _MXU_DTYPE)],
        compiler_params=_cparams(("parallel", "arbitrary")),
        name=name,
    )(x, g.reshape(1, d), w)


def _s5_kern_body(cq_ref, pd_ref, o_ref):
    for n in range(cq_ref.shape[0]):
        o_ref[n] = _mm_exact(cq_ref[n], pd_ref[n])


def _s5_tables(lam_re, lam_im, log_step, b_re, b_im, c_re, c_im, d):
    g_n, p_n, gs, ck = S5_GROUPS, S5_STATE, S5_GROUP, S5_CHUNK
    lam = lax.complex(jnp.minimum(lam_re.astype(F32), -1e-4), lam_im.astype(F32))
    delta = jnp.exp(log_step.astype(F32))[..., None]
    lam_bar = jnp.exp(lam * delta)
    b_bar = ((lam_bar - 1.0) / lam)[..., None] * lax.complex(b_re.astype(F32), b_im.astype(F32))
    c = lax.complex(c_re.astype(F32), c_im.astype(F32))
    pows = [jnp.ones_like(lam_bar)]
    for _ in range(ck):
        pows.append(pows[-1] * lam_bar)
    pw = jnp.stack(pows)
    pw_rev = jnp.stack(pows[::-1])

    x = pw[:ck, :, :, :, None] * b_bar[None]
    xt = x.transpose(1, 2, 3, 0, 4).reshape(2, g_n, p_n, ck * gs)
    pd_t = jnp.concatenate([jnp.real(xt), jnp.imag(xt)], axis=2).reshape(2 * g_n, 2 * p_n, ck * gs)
    cq_t = jnp.concatenate([jnp.real(c), -jnp.imag(c)], axis=-1).reshape(2 * g_n, gs, 2 * p_n)
    gb = 8
    kt = pl.pallas_call(
        _s5_kern_body,
        grid=(2 * g_n // gb,),
        in_specs=[
            pl.BlockSpec((gb, gs, 2 * p_n), lambda i: (i, 0, 0)),
            pl.BlockSpec((gb, 2 * p_n, ck * gs), lambda i: (i, 0, 0)),
        ],
        out_specs=pl.BlockSpec((gb, gs, ck * gs), lambda i: (i, 0, 0)),
        out_shape=jax.ShapeDtypeStruct((2 * g_n, gs, ck * gs), F32),
        compiler_params=_cparams(("parallel",)),
        name="s5_kernels",
    )(cq_t, pd_t)
    kt = kt.reshape(2, g_n, gs, ck, gs)
    s_i = jnp.arange(ck)[:, None]
    t_i = jnp.arange(ck)[None, :]
    dfw = t_i - s_i
    dbw = s_i - t_i
    tf = jnp.where((dfw >= 0)[None, None, :, :, None], kt[0][:, :, jnp.clip(dfw, 0, ck - 1), :], 0.0)
    tb = jnp.where((dbw >= 0)[None, None, :, :, None], kt[1][:, :, jnp.clip(dbw, 0, ck - 1), :], 0.0)
    toep = (tf + tb).transpose(0, 2, 4, 3, 1).reshape(g_n, ck * gs, ck * gs)

    def reim_last(z):
        return jnp.concatenate([jnp.real(z), jnp.imag(z)], axis=-1)

    xf = pw_rev[1:ck + 1, 0, :, :, None] * b_bar[0][None]
    xf = xf.transpose(1, 0, 3, 2).reshape(g_n, ck * gs, p_n)
    xb = x[:, 1].transpose(1, 0, 3, 2).reshape(g_n, ck * gs, p_n)
    w_state = jnp.concatenate([reim_last(xf), reim_last(xb)], axis=-1)

    zf = (c[0][None] * pw[1:ck + 1, 0][:, :, None, :]).transpose(1, 3, 0, 2).reshape(g_n, p_n, ck * gs)
    zb = (c[1][None] * pw_rev[:ck, 1][:, :, None, :]).transpose(1, 3, 0, 2).reshape(g_n, p_n, ck * gs)
    wo_f = jnp.concatenate([jnp.real(zf), -jnp.imag(zf)], axis=1)
    wo_b = jnp.concatenate([jnp.real(zb), -jnp.imag(zb)], axis=1)

    def coef(a):
        return (jnp.concatenate([jnp.real(a), jnp.real(a)], -1), jnp.concatenate([-jnp.imag(a), jnp.imag(a)], -1))

    a_f = coef(pw[ck, 0])
    a_b = coef(pw[ck, 1])
    mx = _MXU_DTYPE
    n8, gl = g_n // S5_GL, S5_GL
    kk = ck * gl * gs

    def rows_sgc(w):
        n = w.shape[-1]
        return w.astype(mx).reshape(n8, gl, ck, gs, n).transpose(0, 2, 1, 3, 4).reshape(n8, kk, n)

    def rows_gp(w):
        return w.astype(mx).reshape(n8, gl * w.shape[1], w.shape[2])

    td = (ck, gs)
    ep = (2, 2 * p_n)
    return dict(toep=_s5_widen(rows_sgc(toep), gs, td), w_state=_s5_widen(rows_sgc(w_state), gs, ep),
                wo_f=_s5_widen(rows_gp(wo_f), 2 * p_n, td), wo_b=_s5_widen(rows_gp(wo_b), 2 * p_n, td),
                a_f=a_f, a_b=a_b, d=d.astype(F32).reshape(1, g_n * gs))


def _s5_widen_body(w_ref, e_ref, o_ref, *, row_div, col_div):
    y = _mm(w_ref[0], e_ref[...])
    r_g = (lax.broadcasted_iota(jnp.int32, y.shape, 0) // row_div) % S5_GL
    c_g = ((lax.broadcasted_iota(jnp.int32, y.shape, 1) + pl.program_id(1) * y.shape[1]) // col_div) % S5_GL
    o_ref[0] = jnp.where(r_g == c_g, y, 0.0).astype(o_ref.dtype)


def _s5_widen(w, row_div, col_dims):
    n8, rows, n = w.shape
    a, b = col_dims
    e = jnp.broadcast_to(jnp.eye(a * b, dtype=w.dtype).reshape(a, b, a, 1, b), (a, b, a, S5_GL, b))
    e = e.reshape(n, n * S5_GL)
    tn = 512
    return pl.pallas_call(
        functools.partial(_s5_widen_body, row_div=row_div, col_div=b),
        grid=(n8, n * S5_GL // tn),
        in_specs=[pl.BlockSpec((1, rows, n), lambda x, j: (x, 0, 0)), pl.BlockSpec((n, tn), lambda x, j: (0, j))],
        out_specs=pl.BlockSpec((1, rows, tn), lambda x, j: (x, 0, j)),
        out_shape=jax.ShapeDtypeStruct((n8, rows, n * S5_GL), w.dtype),
        compiler_params=_cparams(("parallel", "arbitrary")),
        name="s5_widen",
    )(w, e)


def _s5_gather_chunks(x_ref, xc_ref):
    rt = xc_ref.shape[0]
    parts = [x_ref[pl.ds(s, rt, stride=S5_CHUNK), :] for s in range(S5_CHUNK)]
    xc_ref[...] = jnp.concatenate(parts, axis=1).astype(xc_ref.dtype)


def _s5_state_in_body(u_ref, w_ref, vf_ref, vb_ref, xc_ref, pf_ref, pb_ref):
    _s5_gather_chunks(u_ref, xc_ref)
    rt = xc_ref.shape[0]
    v = _mm(xc_ref[...], w_ref[0])
    for n, (o_ref, p_ref) in enumerate(((vf_ref, pf_ref), (vb_ref, pb_ref))):
        for gi in range(S5_GL):
            c0 = (n * S5_GL + gi) * S5_ST2
            p_ref[pl.ds(gi, rt, stride=S5_GL), :] = v[:, c0:c0 + S5_ST2]
        o_ref[...] = p_ref[...].reshape(o_ref.shape)


def _s5_scan_body(vf_ref, vb_ref, afr_ref, afi_ref, abr_ref, abi_ref, sf_ref, sb_ref, cf_ref, cb_ref, *, rt):
    @pl.when(pl.program_id(1) == 0)
    def _():
        cf_ref[...] = jnp.zeros_like(cf_ref)
        cb_ref[...] = jnp.zeros_like(cb_ref)

    afr, afi, abr, abi = afr_ref[...], afi_ref[...], abr_ref[...], abi_ref[...]

    def step(k, carry):
        s_f, s_b = carry
        sf_ref[0, k] = s_f.astype(sf_ref.dtype)
        s_f = afr * s_f + afi * pltpu.roll(s_f, S5_STATE, 1) + vf_ref[0, k]
        kb = rt - 1 - k
        sb_ref[0, kb] = s_b.astype(sb_ref.dtype)
        s_b = abr * s_b + abi * pltpu.roll(s_b, S5_STATE, 1) + vb_ref[0, kb]
        return s_f, s_b

    s_f, s_b = lax.fori_loop(0, rt, step, (cf_ref[...], cb_ref[...]))
    cf_ref[...] = s_f
    cb_ref[...] = s_b


def _s5_out_body(u_ref, t_ref, sf_ref, sb_ref, wof_ref, wob_ref, d_ref, o_ref, xc_ref, pf_ref, pb_ref):
    _s5_gather_chunks(u_ref, xc_ref)
    rt = xc_ref.shape[0]
    mxd = xc_ref.dtype

    def by_group(s_ref, p_ref):
        p_ref[...] = s_ref[...].reshape(p_ref.shape)
        return jnp.concatenate([p_ref[pl.ds(gi, rt, stride=S5_GL), :] for gi in range(S5_GL)], axis=1).astype(mxd)

    s_f = by_group(sf_ref, pf_ref)
    s_b = by_group(sb_ref, pb_ref)
    xc = xc_ref[...]
    pw = S5_PANEL * 128
    for p0 in range(0, S5_CHUNK, S5_PANEL):
        cols = pl.ds(p0 * 128, pw)
        y = _mm(xc, t_ref[0, :, cols]) + _mm(s_f, wof_ref[0, :, cols]) + _mm(s_b, wob_ref[0, :, cols])
        for s in range(S5_PANEL):
            rows = pl.ds(p0 + s, rt, stride=S5_CHUNK)
            o_ref[rows, :] = jax.nn.gelu(y[:, s * 128:(s + 1) * 128] + d_ref[...] * u_ref[rows, :])


def _glu_body(y_ref, wv_ref, wg_ref, x_ref, o_ref, yb_ref):
    @pl.when(pl.program_id(1) == 0)
    def _():
        yb_ref[...] = y_ref[...].astype(yb_ref.dtype)

    y = yb_ref[...]
    val = _mm(y, wv_ref[...])
    gate = _mm(y, wg_ref[...])
    o_ref[...] = x_ref[...] + val * _sigmoid(gate)


def _s5_mixer(x, bsz, length, g, w_in, tabs, w_glu):
    t, d = x.shape
    g_n, ck = S5_GROUPS, S5_CHUNK
    r_n = length // ck
    rows = r_n * bsz
    mx = _MXU_DTYPE
    n8 = g_n // S5_GL
    lanes = S5_GL * S5_GROUP
    k8 = ck * lanes
    st8 = S5_GL * S5_ST2
    u = _rms_mm(x, g, w_in, "s5_in", out_dtype=F32)

    rb = _pick(rows, 256)
    u_spec = pl.BlockSpec((rb * ck, lanes), lambda j, i: (i, j))
    st_spec = lambda: pl.BlockSpec((rb, S5_GL, S5_ST2), lambda j, i: (i, j, 0))
    held = lambda shape: pl.BlockSpec((1,) + shape, lambda j, i: (j, 0, 0))
    vf, vb = pl.pallas_call(
        _s5_state_in_body,
        grid=(n8, rows // rb),
        in_specs=[u_spec, held((k8, 2 * st8))],
        out_specs=[st_spec(), st_spec()],
        out_shape=[jax.ShapeDtypeStruct((rows, g_n, S5_ST2), F32)] * 2,
        scratch_shapes=[pltpu.VMEM((rb, k8), mx)] + [pltpu.VMEM((rb * S5_GL, S5_ST2), F32)] * 2,
        compiler_params=_cparams(("parallel", "arbitrary")),
        name="s5_state_in",
    )(u, tabs["w_state"])

    rt = _pick(r_n, 16)
    n_rt = r_n // rt
    v4 = (bsz, r_n, g_n, S5_ST2)
    blk = (1, rt, g_n, S5_ST2)
    fwd = lambda b, i: (b, i, 0, 0)
    bwd = lambda b, i: (b, n_rt - 1 - i, 0, 0)
    coef_spec = pl.BlockSpec((g_n, S5_ST2), lambda b, i: (0, 0))
    sf, sb = pl.pallas_call(
        functools.partial(_s5_scan_body, rt=rt),
        grid=(bsz, n_rt),
        in_specs=[pl.BlockSpec(blk, fwd), pl.BlockSpec(blk, bwd), coef_spec, coef_spec, coef_spec, coef_spec],
        out_specs=[pl.BlockSpec(blk, fwd), pl.BlockSpec(blk, bwd)],
        out_shape=[jax.ShapeDtypeStruct(v4, F32)] * 2,
        scratch_shapes=[pltpu.VMEM((g_n, S5_ST2), F32)] * 2,
        compiler_params=_cparams(("parallel", "arbitrary")),
        name="s5_scan",
    )(vf.reshape(v4), vb.reshape(v4), tabs["a_f"][0], tabs["a_f"][1], tabs["a_b"][0], tabs["a_b"][1])
    sf = sf.reshape(rows, g_n, S5_ST2)
    sb = sb.reshape(rows, g_n, S5_ST2)

    y = pl.pallas_call(
        _s5_out_body,
        grid=(n8, rows // rb),
        in_specs=[u_spec, held((k8, k8)), st_spec(), st_spec(), held((st8, k8)), held((st8, k8)),
                  pl.BlockSpec((1, lanes), lambda j, i: (0, j))],
        out_specs=u_spec,
        out_shape=jax.ShapeDtypeStruct((t, d), F32),
        scratch_shapes=[pltpu.VMEM((rb, k8), mx)] + [pltpu.VMEM((rb * S5_GL, S5_ST2), F32)] * 2,
        compiler_params=_cparams(("parallel", "arbitrary")),
        name="s5_out",
    )(u, tabs["toep"], sf, sb, tabs["wo_f"], tabs["wo_b"], tabs["d"])

    tm = _pick(t, 1024)
    tn = 512
    n_j = d // tn
    return pl.pallas_call(
        _glu_body,
        grid=(t // tm, n_j),
        in_specs=[
            pl.BlockSpec((tm, d), lambda i, j: (i, 0)),
            pl.BlockSpec((d, tn), lambda i, j: (0, j)),
            pl.BlockSpec((d, tn), lambda i, j: (0, j + n_j)),
            pl.BlockSpec((tm, tn), lambda i, j: (i, j)),
        ],
        out_specs=pl.BlockSpec((tm, tn), lambda i, j: (i, j)),
        out_shape=jax.ShapeDtypeStruct((t, d), F32),
        scratch_shapes=[pltpu.VMEM((tm, d), mx)],
        compiler_params=_cparams(("parallel", "arbitrary")),
        name="s5_glu",
    )(y, w_glu, w_glu, x)


def _log_sigmoid(x):
    return jnp.minimum(x, 0.0) - jnp.log(1.0 + jnp.exp(-jnp.abs(x)))


def _ml_pre_body(xm_ref, prev_ref, next_ref, cw_ref, cb_ref, bdq_ref, bdkt_ref, bdv_ref, wgt_ref, bgt_ref,
                 qv_ref, kt_ref, xc_ref, gatest_ref, pt_ref, st_ref, ext_ref, *, tiles_per_seq):
    tm = xm_ref.shape[0]
    mxd = xm_ref.dtype
    pos = lax.rem(pl.program_id(0), tiles_per_seq)
    ext_ref[pl.ds(HALO, tm), :] = xm_ref[...].astype(F32)
    ext_ref[pl.ds(0, HALO), :] = jnp.where(pos == 0, 0.0, prev_ref[...].astype(F32))
    ext_ref[pl.ds(HALO + tm, HALO), :] = jnp.where(pos == tiles_per_seq - 1, 0.0, next_ref[...].astype(F32))
    nt = (((1,), (1,)), ((), ()))
    scale = ML_DH ** -0.5

    def conv(h):
        sl = pl.ds(h * ML_DH, ML_DH)
        pre = cb_ref[:, sl]
        for jj in range(ML_CONV):
            pre = pre + cw_ref[jj:jj + 1, sl] * ext_ref[pl.ds(HALO - ML_CONV // 2 + jj, tm), sl]
        xc = (pre * _sigmoid(pre)).astype(mxd)
        xc_ref[:, sl] = xc
        return xc

    def project(h, xc):
        sl = pl.ds(h * ML_DH, ML_DH)
        qh = _mm(xc, bdq_ref[h]).astype(mxd)
        vh = _mm(xm_ref[:, sl], bdv_ref[h]).astype(mxd)
        kt = (lax.dot_general(bdkt_ref[h], xc, nt, preferred_element_type=F32) * scale).astype(mxd)
        c0 = (h // ML_HP) * 2 * ML_HP * ML_DH + (h % ML_HP) * ML_DH
        qv_ref[:, pl.ds(c0, ML_DH)] = qh
        qv_ref[:, pl.ds(c0 + ML_HP * ML_DH, ML_DH)] = vh
        for c in range(tm // ML_CHUNK):
            kt_ref[c, sl, :] = kt[:, c * ML_CHUNK:(c + 1) * ML_CHUNK]
        return (lax.dot_general(wgt_ref[0, :, sl], qh, nt, preferred_element_type=F32)
                + _mm(wgt_ref[1, :, sl], kt) * (1.0 / scale)
                + lax.dot_general(wgt_ref[2, :, sl], vh, nt, preferred_element_type=F32))

    gates_t = jnp.broadcast_to(bgt_ref[...], (ML_NGATE, tm))
    xc_prev = conv(0)
    for h in range(1, ML_HEADS):
        xc_next = conv(h)
        gates_t = gates_t + project(h - 1, xc_prev)
        xc_prev = xc_next
    gates_t = gates_t + project(ML_HEADS - 1, xc_prev)

    r_i = lax.broadcasted_iota(jnp.int32, (ML_CHUNK, ML_CHUNK), 0)
    c_i = lax.broadcasted_iota(jnp.int32, (ML_CHUNK, ML_CHUNK), 1)
    incl_before = (r_i <= c_i).astype(F32)
    incl_after = (r_i >= c_i).astype(F32)
    gatest_ref[...] = gates_t
    for c in range(tm // ML_CHUNK):
        sl = pl.ds(c * ML_CHUNK, ML_CHUNK)
        lst = _log_sigmoid(gates_t[:, c * ML_CHUNK:(c + 1) * ML_CHUNK])
        pt_ref[:, sl] = _mm_exact(lst, incl_before)
        st_ref[:, sl] = _mm_exact(lst, incl_after)


def _mlstm_body(qvf, kf, rowf, colf, qvb, kb, rowb, colb, hf_ref, hb_ref, m_ref, *c_refs):
    @pl.when(pl.program_id(2) == 0)
    def _():
        m_ref[...] = jnp.zeros_like(m_ref)
        for c_ref in c_refs:
            c_ref[...] = jnp.zeros_like(c_ref)

    qv = (qvf, qvb)
    q_of = lambda d, hd: qv[d][:, pl.ds(hd * ML_DH, ML_DH)]
    v_of = lambda d, hd: qv[d][:, pl.ds((ML_HP + hd) * ML_DH, ML_DH)]
    io = ((None, kf, None, rowf, hf_ref, colf), (None, kb, None, rowb, hb_ref, colb))
    chains = [(hd, d) for hd in range(ML_HP) for d in (0, 1)]
    t_i = lax.broadcasted_iota(jnp.int32, (ML_CHUNK, ML_CHUNK), 0)
    s_i = lax.broadcasted_iota(jnp.int32, (ML_CHUNK, ML_CHUNK), 1)
    masks = (s_i <= t_i, s_i >= t_i)
    m_all = m_ref[...]
    mxd = qvf.dtype
    sq = (ML_CHUNK, ML_AUG)
    ones_blk = jnp.ones(sq, mxd)

    st = {}

    def run_stages(sub):
        for i, (hd, d) in sub:
            row = io[d][3][hd]
            col = io[d][5][:, hd * 4:(hd + 1) * 4]
            bc = col[:, d:d + 1]
            lic = col[:, 2 + d:3 + d]
            bcr = row[d:d + 1, :]
            lir = row[2 + d:3 + d, :]
            g = bcr[:, ML_CHUNK - 1:ML_CHUNK] if d == 0 else bcr[:, 0:1]
            m_prev = m_all[i:i + 1, 0:1]
            bc_b = jnp.broadcast_to(bc, sq)
            a_b = bc_b + m_prev
            drow = lir - bcr
            dm = jnp.where(masks[d], bc_b + drow, -jnp.inf)
            m_t = jnp.maximum(a_b, jnp.max(dm, axis=1, keepdims=True))
            dec = jnp.exp(dm - m_t)
            e = jnp.exp(a_b - m_t)
            m_new = jnp.maximum(g + m_prev, jnp.max(g + drow, axis=1, keepdims=True))
            decay = jnp.exp(g + m_prev - m_new)
            wr = jnp.exp(g - bc_b + jnp.broadcast_to(lic, sq) - m_new)
            st[i] = dict(m_t=m_t, dec=dec, e=e, m_new=m_new, decay=decay, wr=wr)

        for i, (hd, d) in sub:
            sl = pl.ds(hd * ML_DH, ML_DH)
            s = _mm(q_of(d, hd), io[d][1][sl, :]) * st[i]["dec"]
            st[i]["s"] = s.astype(mxd)

        for i, (hd, d) in sub:
            sl = pl.ds(hd * ML_DH, ML_DH)
            v_aug = jnp.concatenate([v_of(d, hd), ones_blk], axis=1)
            e3 = jnp.concatenate([st[i]["e"]] * (ML_DH // ML_AUG + 1), axis=1)
            nd = e3 * _mm(q_of(d, hd), c_refs[i][...].astype(mxd)) + _mm(st[i]["s"], v_aug)
            inv = 1.0 / jnp.maximum(jnp.abs(nd[:, ML_DH:]), jnp.exp(-st[i]["m_t"]))
            inv2 = jnp.concatenate([inv] * (ML_DH // ML_AUG), axis=1)
            io[d][4][:, sl] = (nd[:, :ML_DH] * inv2).astype(io[d][4].dtype)

        for i, (hd, d) in sub:
            sl = pl.ds(hd * ML_DH, ML_DH)
            v_aug = jnp.concatenate([v_of(d, hd), ones_blk], axis=1)
            wr3 = jnp.concatenate([st[i]["wr"]] * (ML_DH // ML_AUG + 1), axis=1)
            vw = (v_aug.astype(F32) * wr3).astype(mxd)
            c_refs[i][...] = st[i]["decay"] * c_refs[i][...] + _mm(io[d][1][sl, :], vw)

    for g0 in range(0, len(chains), ML_GRP):
        run_stages(list(enumerate(chains))[g0:g0 + ML_GRP])
    m_ref[...] = jnp.concatenate([jnp.broadcast_to(st[i]["m_new"], (1, m_all.shape[1]))
                                  for i in range(len(chains))], axis=0)


def _ml_out_body(hf_ref, hb_ref, z_ref, xc_ref, ng_ref, sk_ref, w_ref, x_ref, o_ref):
    @pl.when(pl.program_id(1) == 0)
    def _():
        o_ref[...] = x_ref[...]

    acc = o_ref[...]
    for hd in range(ML_OG):
        sl = pl.ds(hd * ML_DH, ML_DH)
        hh = hf_ref[:, sl].astype(F32) + hb_ref[:, sl].astype(F32)
        mu = jnp.mean(hh, axis=-1, keepdims=True)
        cen = hh - mu
        var = jnp.mean(cen * cen, axis=-1, keepdims=True)
        hn = cen * lax.rsqrt(var + EPS) * ng_ref[:, sl]
        out = _sigmoid(z_ref[:, sl].astype(F32)) * (hn + sk_ref[:, sl] * xc_ref[:, sl].astype(F32))
        acc = acc + _mm(out.astype(w_ref.dtype), w_ref[sl, :])
    o_ref[...] = acc


def _blockdiag(w):
    nb = ML_DH // ML_BLK
    wt = w.astype(F32).reshape(ML_HEADS, nb, ML_BLK, ML_BLK)
    eye = jnp.eye(nb, dtype=F32)
    return jnp.einsum("hncd,nm->hncmd", wt, eye).reshape(ML_HEADS, ML_DH, ML_DH).astype(_MXU_DTYPE)


def _mlstm_mixer(x, bsz, length, g, w_in, conv_w, conv_b, wq, wk, wv, w_gates, b_gates, norm_g, skip, w_out):
    t, d = x.shape
    mx = _MXU_DTYPE
    di, dh, nh = ML_INNER, ML_DH, ML_HEADS
    xz = _rms_mm(x, g, w_in, "ml_in")

    tm = _pick(length, 256)
    tiles_per_seq = length // tm
    hb = tm // HALO
    n_h = t // HALO
    wgt = w_gates.astype(mx).transpose(0, 2, 1)
    row_spec = lambda: pl.BlockSpec((tm, di), lambda i: (i, 0))
    small_t = lambda: pl.BlockSpec((ML_NGATE, tm), lambda i: (0, i))
    whole = lambda a: pl.BlockSpec(a.shape, lambda i: (0,) * a.ndim)
    consts = (conv_w.astype(F32), conv_b.astype(F32).reshape(1, di), _blockdiag(wq),
              _blockdiag(wk).transpose(0, 2, 1), _blockdiag(wv), wgt, b_gates.astype(F32).reshape(ML_NGATE, 1))
    outs = pl.pallas_call(
        functools.partial(_ml_pre_body, tiles_per_seq=tiles_per_seq),
        grid=(t // tm,),
        in_specs=[
            row_spec(),
            pl.BlockSpec((HALO, di), lambda i: (jnp.maximum(i * hb - 1, 0), 0)),
            pl.BlockSpec((HALO, di), lambda i: (jnp.minimum((i + 1) * hb, n_h - 1), 0)),
        ] + [whole(a) for a in consts],
        out_specs=[pl.BlockSpec((tm, 2 * di), lambda i: (i, 0)),
                   pl.BlockSpec((tm // ML_CHUNK, di, ML_CHUNK), lambda i: (i, 0, 0)),
                   row_spec(), small_t(), small_t(), small_t()],
        out_shape=[jax.ShapeDtypeStruct((t, 2 * di), mx), jax.ShapeDtypeStruct((t // ML_CHUNK, di, ML_CHUNK), mx),
                   jax.ShapeDtypeStruct((t, di), mx)] + [jax.ShapeDtypeStruct((ML_NGATE, t), F32)] * 3,
        scratch_shapes=[pltpu.VMEM((tm + 2 * HALO, di), F32)],
        compiler_params=_cparams(("parallel",)),
        name="ml_pre",
    )(xz, xz, xz, *consts)
    qv, kt, xc, gates_t, pre_t, suf_t = outs
    rows = jnp.stack([pre_t[nh:2 * nh], suf_t[3 * nh:], gates_t[:nh], gates_t[2 * nh:3 * nh]], axis=1)
    cols = rows.reshape(nh // ML_HP, ML_HP, 4, t).transpose(0, 3, 1, 2).reshape(nh // ML_HP, t, ML_HP * 4)

    nc = length // ML_CHUNK
    hp = ML_HP
    fw = lambda b, h, j: (b * nc + j, h)
    bw = lambda b, h, j: (b * nc + nc - 1 - j, h)
    qkv = lambda f: pl.BlockSpec((ML_CHUNK, hp * dh), f)
    qvspec = lambda f: pl.BlockSpec((ML_CHUNK, 2 * hp * dh), f)
    ktspec = lambda f: pl.BlockSpec((None, hp * dh, ML_CHUNK), lambda b, h, j: f(b, h, j) + (0,))
    rowspec = lambda f: pl.BlockSpec((hp, 4, ML_CHUNK), lambda b, h, j: (h, 0, f(b, h, j)[0]))
    colspec = lambda f: pl.BlockSpec((None, ML_CHUNK, hp * 4), lambda b, h, j: (h, f(b, h, j)[0], 0))
    hf, hbk = pl.pallas_call(
        _mlstm_body,
        grid=(bsz, nh // hp, nc),
        in_specs=[qvspec(fw), ktspec(fw), rowspec(fw), colspec(fw),
                  qvspec(bw), ktspec(bw), rowspec(bw), colspec(bw)],
        out_specs=[qkv(fw), qkv(bw)],
        out_shape=[jax.ShapeDtypeStruct((t, di), mx)] * 2,
        scratch_shapes=[pltpu.VMEM((2 * hp, 128), F32)] + [pltpu.VMEM((dh, dh + ML_AUG), F32)] * (2 * hp),
        compiler_params=_cparams(("parallel", "parallel", "arbitrary")),
        name="mlstm",
    )(qv, kt, rows, cols, qv, kt, rows, cols)

    tmo = _pick(t, 512)
    kg = ML_OG * dh
    n_g = nh // ML_OG
    grp = lambda: pl.BlockSpec((tmo, kg), lambda i, j: (i, j))
    return pl.pallas_call(
        _ml_out_body,
        grid=(t // tmo, n_g),
        in_specs=[
            grp(), grp(),
            pl.BlockSpec((tmo, kg), lambda i, j: (i, j + n_g)),
            grp(),
            pl.BlockSpec((1, kg), lambda i, j: (0, j)),
            pl.BlockSpec((1, kg), lambda i, j: (0, j)),
            pl.BlockSpec((kg, d), lambda i, j: (j, 0)),
            pl.BlockSpec((tmo, d), lambda i, j: (i, 0)),
        ],
        out_specs=pl.BlockSpec((tmo, d), lambda i, j: (i, 0)),
        out_shape=jax.ShapeDtypeStruct((t, d), F32),
        compiler_params=_cparams(("parallel", "arbitrary")),
        name="ml_out",
    )(hf, hbk, xz, xc, norm_g.astype(F32).reshape(1, di), skip.astype(F32).reshape(1, di), w_out, x)


def _trunk(x3, p):
    bsz, length, d = x3.shape
    x = x3.reshape(bsz * length, d)
    ffn = lambda x, layer, idx, **kw: _ffn(x, p["norm_g"][layer, 2 * idx], p["ffn_w_in"], p["ffn_w_out"],
                                           2 * layer + idx, **kw)
    x = ffn(x, 0, 0)
    x = _s5_mixer(x, bsz, length, p["norm_g"][0, 1], p["s5_w_in"], p["s5_tabs"], p["s5_w_glu"])
    x = ffn(x, 0, 1)
    x = ffn(x, 1, 0)
    x = _mlstm_mixer(x, bsz, length, p["norm_g"][1, 1], p["ml_w_in"], *p["ml_rest"])
    x = ffn(x, 1, 1, final_g=p["final_g"])
    return x.reshape(bsz, length, d)


def kernel(x_prompt, x_sample, norm_g, final_g, ffn_w_in, ffn_w_out, s5_w_in, s5_lambda_re, s5_lambda_im, s5_log_step, s5_b_re, s5_b_im, s5_c_re, s5_c_im, s5_d, s5_w_glu, ml_w_in, ml_conv_w, ml_conv_b, ml_wq, ml_wk, ml_wv, ml_w_gates, ml_b_gates, ml_norm_g, ml_skip, ml_w_out):
    mx = _MXU_DTYPE
    p = dict(
        norm_g=norm_g.astype(F32), final_g=final_g.astype(F32),
        ffn_w_in=ffn_w_in.astype(mx).reshape(-1, D_MODEL, 2 * D_FF),
        ffn_w_out=ffn_w_out.astype(mx).reshape(-1, D_FF, D_MODEL),
        s5_w_in=s5_w_in[0].astype(mx), s5_w_glu=s5_w_glu[0].astype(mx),
        s5_tabs=_s5_tables(s5_lambda_re[0], s5_lambda_im[0], s5_log_step[0], s5_b_re[0], s5_b_im[0], s5_c_re[0],
                           s5_c_im[0], s5_d[0]),
        ml_w_in=ml_w_in[0].astype(mx),
        ml_rest=(ml_conv_w[0], ml_conv_b[0], ml_wq[0], ml_wk[0], ml_wv[0], ml_w_gates[0], ml_b_gates[0],
                 ml_norm_g[0], ml_skip[0], ml_w_out[0].astype(mx)),
    )
    return (_trunk(x_prompt, p), _trunk(x_sample, p))
```
